```python
import math
import jax, jax.numpy as jnp
from jax import lax
import numpy as np

D_MODEL = 1024
BATCH = 8
SEQ = 4096
DEPTH = 4

CHUNK = 64
N_EVEN = (DEPTH + 1) // 2
N_ODD = DEPTH // 2
SSM_INNER = 3 * D_MODEL // 2
SSM_HEAD_DIM = 64
SSM_HEADS = SSM_INNER // SSM_HEAD_DIM
SSM_GROUPS = 4
SSM_HPG = SSM_HEADS // SSM_GROUPS
SSM_STATE = 128
SSM_CONV = 4
SSD_CHUNK = 64
SSM_CONV_CH = SSM_INNER + 2 * SSM_GROUPS * SSM_STATE
SCONV_DIM = D_MODEL // 2
SCONV_WIDTH = 3
EVEN_IN = SSM_INNER + SSM_CONV_CH + SSM_HEADS + 3 * SCONV_DIM
EVEN_MIX = SSM_INNER + SCONV_DIM
ATT_HEADS = 8
ATT_HEAD_DIM = 64
ATT_DIM = ATT_HEADS * 2 * ATT_HEAD_DIM
ROPE_DIM = ATT_HEAD_DIM // 4
ROPE_THETA = 500000.0
Q_BLOCK = 128
MOE_GROUPS = 8
EXPERTS_PER_GROUP = 8
N_EXPERTS = MOE_GROUPS * EXPERTS_PER_GROUP
TOP_K = 2
D_EXPERT = D_MODEL // 2
MOE_BLOCK = 128
EPS = 1e-6

kernel_name = "hybrid_ssd_shortconv_diffattn_hmoe"


def rms_norm(x, w):
    xf = x.astype(jnp.float32)
    y = xf * lax.rsqrt(jnp.mean(xf * xf, axis=-1, keepdims=True) + EPS)
    return (y * w.astype(jnp.float32)).astype(x.dtype)


def causal_dwconv(x, w):
    k = w.shape[0]
    return lax.conv_general_dilated(
        x, w[:, None, :], window_strides=(1,), padding=[(k - 1, 0)],
        dimension_numbers=("NWC", "WIO", "NWC"), feature_group_count=x.shape[-1])


def ssd_chunk_scan(xdt, a, bm, cm):
    b, t = xdt.shape[:2]
    nc = t // SSD_CHUNK

    def to_chunks(v):
        return jnp.moveaxis(v.reshape(b, nc, SSD_CHUNK, *v.shape[2:]), 1, 0)

    tril = jnp.tril(jnp.ones((SSD_CHUNK, SSD_CHUNK), bool))[None, :, :, None, None]

    def step(state, inp):
        xc, ac, bc, cc = inp
        acum = jnp.cumsum(ac, axis=1)
        seg = acum[:, :, None] - acum[:, None, :]
        lmat = jnp.exp(jnp.where(tril, seg, -jnp.inf))
        cb = jnp.einsum("blgn,bsgn->blsg", cc, bc)
        y = jnp.einsum("blsg,blsgr,bsgrp->blgrp", cb, lmat, xc)
        y = y + jnp.einsum("blgn,bgrpn->blgrp", cc, state) * jnp.exp(acum)[..., None]
        decay = jnp.exp(acum[:, -1:] - acum)
        state = state * jnp.exp(acum[:, -1])[..., None, None] + jnp.einsum(
            "blgn,blgr,blgrp->bgrpn", bc, decay, xc)
        return state, y

    s0 = jnp.zeros((b, SSM_GROUPS, SSM_HPG, SSM_HEAD_DIM, SSM_STATE), jnp.float32)
    _, ys = lax.scan(step, s0, (to_chunks(xdt), to_chunks(a), to_chunks(bm), to_chunks(cm)))
    return jnp.moveaxis(ys, 0, 1).reshape(b, t, SSM_GROUPS, SSM_HPG, SSM_HEAD_DIM)


def even_mixer(h, w_in, conv_w, conv_b, dt_bias, a_log, d_skip, gnorm_w, sconv_w, w_out):
    b, t, _ = h.shape
    proj = h @ w_in
    o1 = SSM_INNER
    o2 = o1 + SSM_CONV_CH
    o3 = o2 + SSM_HEADS
    o4 = o3 + SCONV_DIM
    o5 = o4 + SCONV_DIM
    z, xbc, dt_raw, hb, bg, cg = jnp.split(proj, [o1, o2, o3, o4, o5], axis=-1)
    xbc = jax.nn.silu(causal_dwconv(xbc, conv_w) + conv_b)
    xs, bm, cm = jnp.split(xbc, [SSM_INNER, SSM_INNER + SSM_GROUPS * SSM_STATE], axis=-1)
    dt = jax.nn.softplus((dt_raw + dt_bias).astype(jnp.float32))
    a_neg = -jnp.exp(a_log.astype(jnp.float32)).reshape(SSM_GROUPS, SSM_HPG)
    xs_h = xs.astype(jnp.float32).reshape(b, t, SSM_GROUPS, SSM_HPG, SSM_HEAD_DIM)
    dt_h = dt.reshape(b, t, SSM_GROUPS, SSM_HPG)
    y = ssd_chunk_scan(xs_h * dt_h[..., None], dt_h * a_neg,
                       bm.astype(jnp.float32).reshape(b, t, SSM_GROUPS, SSM_STATE),
                       cm.astype(jnp.float32).reshape(b, t, SSM_GROUPS, SSM_STATE))
    y = y + d_skip.astype(jnp.float32).reshape(SSM_GROUPS, SSM_HPG)[..., None] * xs_h
    y = y.reshape(b, t, SSM_INNER) * jax.nn.silu(z.astype(jnp.float32))
    y = rms_norm(y.reshape(b, t, SSM_GROUPS, SSM_INNER // SSM_GROUPS),
                 gnorm_w.reshape(SSM_GROUPS, SSM_INNER // SSM_GROUPS))
    ya = y.reshape(b, t, SSM_INNER).astype(h.dtype)
    yb = bg * causal_dwconv(cg * hb, sconv_w)
    return jnp.concatenate([ya, yb], axis=-1) @ w_out


def rope_tables(t):
    inv = 1.0 / (ROPE_THETA ** (jnp.arange(0, ROPE_DIM, 2, dtype=jnp.float32) / ROPE_DIM))
    ang = jnp.arange(t, dtype=jnp.float32)[:, None] * inv[None, :]
    return jnp.cos(ang), jnp.sin(ang)


def apply_partial_rope(v, cos, sin):
    half = ROPE_DIM // 2
    c = cos[None, :, None, None, :]
    s = sin[None, :, None, None, :]
    v1 = v[..., :half]
    v2 = v[..., half:ROPE_DIM]
    return jnp.concatenate([v1 * c - v2 * s, v2 * c + v1 * s, v[..., ROPE_DIM:]], axis=-1)


def diff_attention(h, w_qkv, lam_p, subln_w, w_out, lambda_init, cos, sin):
    b, t, _ = h.shape
    q, k, v = jnp.split(h @ w_qkv, 3, axis=-1)
    q = q.astype(jnp.float32).reshape(b, t, ATT_HEADS, 2, ATT_HEAD_DIM)
    k = k.astype(jnp.float32).reshape(b, t, ATT_HEADS, 2, ATT_HEAD_DIM)
    v = v.astype(jnp.float32).reshape(b, t, ATT_HEADS, 2 * ATT_HEAD_DIM)
    q = apply_partial_rope(q, cos, sin) * (ATT_HEAD_DIM ** -0.5)
    k = apply_partial_rope(k, cos, sin)
    lp = lam_p.astype(jnp.float32)
    lam = jnp.exp(jnp.sum(lp[0] * lp[1])) - jnp.exp(jnp.sum(lp[2] * lp[3])) + lambda_init
    loc = np.arange(Q_BLOCK) // CHUNK
    local_mask = loc[None, :] <= loc[:, None]
    outs = []
    for qb in range(t // Q_BLOCK):
        q0 = qb * Q_BLOCK
        kend = q0 + Q_BLOCK
        s = jnp.einsum("bqhcd,bkhcd->bhcqk", q[:, q0:kend], k[:, :kend])
        mask = np.concatenate([np.ones((Q_BLOCK, q0), bool), local_mask], axis=1)
        p = jax.nn.softmax(jnp.where(mask, s, -jnp.inf), axis=-1)
        amap = p[:, :, 0] - lam * p[:, :, 1]
        outs.append(jnp.einsum("bhqk,bkhe->bqhe", amap, v[:, :kend]))
    o = jnp.concatenate(outs, axis=1)
    o = rms_norm(o, subln_w) * (1.0 - lambda_init)
    return o.reshape(b, t, ATT_DIM).astype(h.dtype) @ w_out


def hier_moe(h, w_group, b_group, w_expert, b_expert, w1, w3, w2):
    b, t, dm = h.shape
    n = b * t
    xf = h.reshape(n, dm)
    g_prob = jax.nn.softmax((xf @ w_group).astype(jnp.float32) + b_group.astype(jnp.float32), axis=-1)
    g_gate, g_idx = lax.top_k(g_prob, 1)
    e_logits = ((xf @ w_expert).astype(jnp.float32) + b_expert.astype(jnp.float32)).reshape(
        n, MOE_GROUPS, EXPERTS_PER_GROUP)
    e_logits = jnp.take_along_axis(e_logits, g_idx[:, :, None], axis=1)[:, 0]
    top_p, top_i = lax.top_k(jax.nn.softmax(e_logits, axis=-1), TOP_K)
    gates = g_gate * top_p / jnp.sum(top_p, axis=-1, keepdims=True)
    expert_id = g_idx * EXPERTS_PER_GROUP + top_i
    n_assign = n * TOP_K
    flat_e = expert_id.reshape(n_assign)
    flat_tok = jnp.repeat(jnp.arange(n, dtype=jnp.int32), TOP_K)
    flat_gate = gates.reshape(n_assign)
    order = jnp.argsort(flat_e)
    se, stok, sg = flat_e[order], flat_tok[order], flat_gate[order]
    counts = jnp.bincount(flat_e, length=N_EXPERTS)
    starts = jnp.cumsum(counts) - counts
    padded = (counts + MOE_BLOCK - 1) // MOE_BLOCK * MOE_BLOCK
    pends = jnp.cumsum(padded)
    pstarts = pends - padded
    dest = pstarts[se] + jnp.arange(n_assign) - starts[se]
    n_rows = n_assign + N_EXPERTS * MOE_BLOCK
    n_blk = n_rows // MOE_BLOCK
    row_tok = jnp.full((n_rows,), n, jnp.int32).at[dest].set(stok)
    row_gate = jnp.zeros((n_rows,), jnp.float32).at[dest].set(sg)
    blk_expert = jnp.minimum(
        jnp.searchsorted(pends, jnp.arange(n_blk) * MOE_BLOCK, side="right"), N_EXPERTS - 1)
    x_rows = jnp.concatenate([xf, jnp.zeros((1, dm), xf.dtype)], axis=0)[row_tok]
    x_rows = x_rows.reshape(n_blk, MOE_BLOCK, dm)

    def expert_block(args):
        xb, e = args
        return (jax.nn.silu(xb @ w1[e]) * (xb @ w3[e])) @ w2[e]

    y_rows = lax.map(expert_block, (x_rows, blk_expert)).reshape(n_rows, dm)
    out = jnp.zeros((n + 1, dm), jnp.float32).at[row_tok].add(
        y_rows.astype(jnp.float32) * row_gate[:, None])[:n]
    return out.reshape(b, t, dm).astype(h.dtype)


def lambda_init_fn(layer):
    return 0.8 - 0.6 * math.exp(-0.3 * layer)


def setup_inputs(seed: int = 0) -> dict:
    key = jax.random.key(seed)
    ks = jax.random.split(key, 24)

    def nrm(k, shape, scale):
        return jax.random.normal(k, shape, jnp.float32) * scale

    dt0 = jnp.exp(jax.random.uniform(ks[6], (N_EVEN, SSM_HEADS), jnp.float32,
                                     math.log(1e-3), math.log(1e-1)))
    return {
        "x": nrm(ks[0], (BATCH, SEQ, D_MODEL), 1.0),
        "norm_mix_w": 1.0 + nrm(ks[1], (DEPTH, D_MODEL), 0.02),
        "norm_ffn_w": 1.0 + nrm(ks[2], (DEPTH, D_MODEL), 0.02),
        "final_norm_w": 1.0 + nrm(ks[3], (D_MODEL,), 0.02),
        "ev_w_in": nrm(ks[4], (N_EVEN, D_MODEL, EVEN_IN), D_MODEL ** -0.5),
        "ev_conv_w": nrm(ks[5], (N_EVEN, SSM_CONV, SSM_CONV_CH), SSM_CONV ** -0.5),
        "ev_conv_b": nrm(ks[7], (N_EVEN, SSM_CONV_CH), 0.02),
        "ev_dt_bias": dt0 + jnp.log(-jnp.expm1(-dt0)),
        "ev_a_log": jnp.log(jax.random.uniform(ks[8], (N_EVEN, SSM_HEADS), jnp.float32, 1.0, 16.0)),
        "ev_d_skip": 1.0 + nrm(ks[9], (N_EVEN, SSM_HEADS), 0.02),
        "ev_gnorm_w": 1.0 + nrm(ks[10], (N_EVEN, SSM_INNER), 0.02),
        "ev_sconv_w": nrm(ks[11], (N_EVEN, SCONV_WIDTH, SCONV_DIM), SCONV_WIDTH ** -0.5),
        "ev_w_out": nrm(ks[12], (N_EVEN, EVEN_MIX, D_MODEL), EVEN_MIX ** -0.5),
        "od_w_qkv": nrm(ks[13], (N_ODD, D_MODEL, 3 * ATT_DIM), D_MODEL ** -0.5),
        "od_lambda": nrm(ks[14], (N_ODD, 4, ATT_HEAD_DIM), 0.1),
        "od_subln_w": 1.0 + nrm(ks[15], (N_ODD, 2 * ATT_HEAD_DIM), 0.02),
        "od_w_out": nrm(ks[16], (N_ODD, ATT_DIM, D_MODEL), ATT_DIM ** -0.5),
        "moe_w_group": nrm(ks[17], (DEPTH, D_MODEL, MOE_GROUPS), D_MODEL ** -0.5),
        "moe_b_group": nrm(ks[18], (DEPTH, MOE_GROUPS), 0.01),
        "moe_w_expert": nrm(ks[19], (DEPTH, D_MODEL, N_EXPERTS), D_MODEL ** -0.5),
        "moe_b_expert": nrm(ks[20], (DEPTH, N_EXPERTS), 0.01),
        "moe_w1": nrm(ks[21], (DEPTH, N_EXPERTS, D_MODEL, D_EXPERT), D_MODEL ** -0.5),
        "moe_w3": nrm(ks[22], (DEPTH, N_EXPERTS, D_MODEL, D_EXPERT), D_MODEL ** -0.5),
        "moe_w2": nrm(ks[23], (DEPTH, N_EXPERTS, D_EXPERT, D_MODEL), D_EXPERT ** -0.5),
    }


def reference(x, norm_mix_w, norm_ffn_w, final_norm_w, ev_w_in, ev_conv_w, ev_conv_b,
              ev_dt_bias, ev_a_log, ev_d_skip, ev_gnorm_w, ev_sconv_w, ev_w_out,
              od_w_qkv, od_lambda, od_subln_w, od_w_out, moe_w_group, moe_b_group,
              moe_w_expert, moe_b_expert, moe_w1, moe_w3, moe_w2):
    cos, sin = rope_tables(x.shape[1])
    h = x
    for layer in range(DEPTH):
        hn = rms_norm(h, norm_mix_w[layer])
        i = layer // 2
        if layer % 2 == 0:
            h = h + even_mixer(hn, ev_w_in[i], ev_conv_w[i], ev_conv_b[i], ev_dt_bias[i],
                               ev_a_log[i], ev_d_skip[i], ev_gnorm_w[i], ev_sconv_w[i], ev_w_out[i])
        else:
            h = h + diff_attention(hn, od_w_qkv[i], od_lambda[i], od_subln_w[i], od_w_out[i],
                                   lambda_init_fn(layer), cos, sin)
        h = h + hier_moe(rms_norm(h, norm_ffn_w[layer]), moe_w_group[layer], moe_b_group[layer],
                         moe_w_expert[layer], moe_b_expert[layer], moe_w1[layer],
                         moe_w3[layer], moe_w2[layer])
    return rms_norm(h, final_norm_w)
```

```python
import functools
import math

import jax
import jax.numpy as jnp
from jax import lax
from jax.experimental import pallas as pl
from jax.experimental.pallas import tpu as pltpu

F32 = jnp.float32
BF16 = jnp.bfloat16

D_MODEL = 1024
CHUNK = 64
SSM_INNER = 1536
SSM_HEAD_DIM = 64
SSM_HEADS = 24
SSM_GROUPS = 4
SSM_STATE = 128
SSM_CONV = 4
SSM_CONV_CH = SSM_INNER + 2 * SSM_GROUPS * SSM_STATE
SSM_PAIRS = SSM_HEADS // 2
PAIRS_PER_GROUP = SSM_PAIRS // SSM_GROUPS
GROUP_CH = SSM_INNER // SSM_GROUPS
SCONV_DIM = 512
SCONV_WIDTH = 3
EVEN_MIX = SSM_INNER + SCONV_DIM
ATT_HEADS = 8
ATT_HEAD_DIM = 64
ATT_DIM = ATT_HEADS * 2 * ATT_HEAD_DIM
ROPE_DIM = ATT_HEAD_DIM // 4
ROPE_THETA = 500000.0
MOE_GROUPS = 8
EXPERTS_PER_GROUP = 8
N_EXPERTS = 64
D_EXPERT = 512
EPS = 1e-6

LANES = 128
SUBLANES = 8
VMEM_LIMIT = 56 * 1024 * 1024

OFF_Z = 0
OFF_XBC = SSM_INNER
OFF_HB = OFF_XBC + SSM_CONV_CH
OFF_BG = OFF_HB + SCONV_DIM
OFF_CG = OFF_BG + SCONV_DIM
OFF_DT = OFF_CG + SCONV_DIM
EVEN_PROJ = OFF_DT + LANES

ROW_TILE = 256
ATT_TILE = 256
MOE_TILE = 512
MOE_BLOCK = 256
ROUTER_LANE0 = MOE_GROUPS


def _params(*sem):
    return pltpu.CompilerParams(dimension_semantics=sem, vmem_limit_bytes=VMEM_LIMIT)


def _rms(x, w):
    return x * lax.rsqrt(jnp.mean(x * x, axis=-1, keepdims=True) + EPS) * w


def _silu(x):
    return x * (1.0 / (1.0 + jnp.exp(-x)))


def _norm_matmul_body(x_ref, nw_ref, w_ref, o_ref, *, col_chunk):
    xn = _rms(x_ref[...], nw_ref[...]).astype(BF16)
    for c0 in range(0, o_ref.shape[-1], col_chunk):
        o_ref[:, c0:c0 + col_chunk] = jnp.dot(
            xn, w_ref[:, c0:c0 + col_chunk], preferred_element_type=F32).astype(o_ref.dtype)


def norm_matmul(x, nw, w_bf16, col_chunk):
    n, d = x.shape
    f = w_bf16.shape[1]
    return pl.pallas_call(
        functools.partial(_norm_matmul_body, col_chunk=col_chunk),
        grid=(n // ROW_TILE,),
        in_specs=[pl.BlockSpec((ROW_TILE, d), lambda i: (i, 0)),
                  pl.BlockSpec((1, d), lambda i: (0, 0)),
                  pl.BlockSpec((d, f), lambda i: (0, 0))],
        out_specs=pl.BlockSpec((ROW_TILE, f), lambda i: (i, 0)),
        out_shape=jax.ShapeDtypeStruct((n, f), F32),
        compiler_params=_params("parallel"),
        name="norm_matmul",
    )(x, nw.reshape(1, d), w_bf16)


def _softplus(x):
    return jnp.maximum(x, 0.0) + jnp.log(1.0 + jnp.exp(-jnp.abs(x)))


def _ssd_body(proj_ref, h_ref, cw_ref, cb_ref, dtb_ref, aneg_ref, dsk_ref, gnw_ref, scw_ref, wout_ref,
              o_ref, cbuf, xact, mix, state, sbuf):
    tc = ROW_TILE
    halo = SUBLANES

    @pl.when(pl.program_id(1) == 0)
    def _():
        state[...] = jnp.zeros_like(state)
        cbuf[0:halo, :] = jnp.zeros((halo, SSM_CONV_CH), F32)
        sbuf[0:halo, :] = jnp.zeros((halo, SCONV_DIM), F32)

    cbuf[halo:halo + tc, :] = proj_ref[0, :, OFF_XBC:OFF_XBC + SSM_CONV_CH]
    for c0 in range(0, SSM_CONV_CH, 512):
        acc = cb_ref[:, c0:c0 + 512]
        for j in range(SSM_CONV):
            lo = halo - (SSM_CONV - 1) + j
            acc = acc + cw_ref[j:j + 1, c0:c0 + 512] * cbuf[lo:lo + tc, c0:c0 + 512]
        xact[:, c0:c0 + 512] = _silu(acc)
    cbuf[0:halo, :] = cbuf[tc:tc + halo, :]

    sbuf[halo:halo + tc, :] = (proj_ref[0, :, OFF_CG:OFF_CG + SCONV_DIM]
                               * proj_ref[0, :, OFF_HB:OFF_HB + SCONV_DIM])
    conv = jnp.zeros((tc, SCONV_DIM), F32)
    for j in range(SCONV_WIDTH):
        lo = halo - (SCONV_WIDTH - 1) + j
        conv = conv + scw_ref[j:j + 1, :] * sbuf[lo:lo + tc, :]
    mix[:, SSM_INNER:EVEN_MIX] = (proj_ref[0, :, OFF_BG:OFF_BG + SCONV_DIM] * conv).astype(BF16)
    sbuf[0:halo, :] = sbuf[tc:tc + halo, :]

    sub = lax.broadcasted_iota(jnp.int32, (CHUNK, LANES), 0)
    lane = lax.broadcasted_iota(jnp.int32, (CHUNK, LANES), 1)
    lane_lo = lane < SSM_HEAD_DIM
    tril2 = sub >= jnp.where(lane_lo, lane, lane - SSM_HEAD_DIM)
    csum_l = lax.broadcasted_iota(jnp.int32, (CHUNK, CHUNK), 0)
    csum_s = lax.broadcasted_iota(jnp.int32, (CHUNK, CHUNK), 1)
    tril_incl = (csum_s <= csum_l).astype(F32)
    sub2 = lax.broadcasted_iota(jnp.int32, (LANES, LANES), 0)
    lane2 = lax.broadcasted_iota(jnp.int32, (LANES, LANES), 1)

    def chunk_body(c, carry):
        r0 = pl.multiple_of(c * CHUNK, CHUNK)
        rows = pl.ds(r0, CHUNK)
        dt = _softplus(proj_ref[0, rows, OFF_DT:OFF_DT + LANES] + dtb_ref[...])
        a = dt * aneg_ref[...]
        acum = jnp.dot(tril_incl, a, preferred_element_type=F32, precision=lax.Precision.HIGHEST)
        acum_t = acum.T
        for g in range(SSM_GROUPS):
            bg = xact[rows, SSM_INNER + g * SSM_STATE:SSM_INNER + (g + 1) * SSM_STATE].astype(BF16)
            c_off = SSM_INNER + SSM_GROUPS * SSM_STATE
            cg = xact[rows, c_off + g * SSM_STATE:c_off + (g + 1) * SSM_STATE].astype(BF16)
            cb = lax.dot_general(cg, bg, (((1,), (1,)), ((), ())), preferred_element_type=F32)
            cb2 = jnp.concatenate([cb, cb], axis=1)
            ys = []
            for jj in range(PAIRS_PER_GROUP):
                j = g * PAIRS_PER_GROUP + jj
                h0, h1 = 2 * j, 2 * j + 1
                xp = xact[rows, j * LANES:(j + 1) * LANES]
                dt2 = jnp.where(lane_lo, dt[:, h0:h0 + 1], dt[:, h1:h1 + 1])
                col2 = jnp.where(lane_lo, acum[:, h0:h0 + 1], acum[:, h1:h1 + 1])
                row2 = jnp.concatenate([acum_t[h0:h0 + 1, :], acum_t[h1:h1 + 1, :]], axis=1)
                last2 = col2[CHUNK - 1:CHUNK, :]
                xdt = xp * dt2
                m2 = cb2 * jnp.exp(jnp.where(tril2, col2 - row2, -jnp.inf))
                rhs = jnp.concatenate([jnp.where(lane_lo, xdt, 0.0), jnp.where(lane_lo, 0.0, xdt)],
                                      axis=0).astype(BF16)
                y2 = jnp.dot(m2.astype(BF16), rhs, preferred_element_type=F32)
                s_old = state[j]
                y_in = lax.dot_general(cg, s_old.astype(BF16), (((1,), (1,)), ((), ())),
                                       preferred_element_type=F32)
                y2 = y2 + y_in * jnp.exp(col2) + dsk_ref[:, j * LANES:(j + 1) * LANES] * xp
                xd = (xdt * jnp.exp(last2 - col2)).astype(BF16)
                upd = lax.dot_general(xd, bg, (((0,), (0,)), ((), ())), preferred_element_type=F32)
                e0 = jnp.exp(acum_t[h0:h0 + 1, CHUNK - 1:CHUNK])
                e1 = jnp.exp(acum_t[h1:h1 + 1, CHUNK - 1:CHUNK])
                state[j] = s_old * jnp.where(sub2 < SSM_HEAD_DIM, e0, e1) + upd
                ys.append(y2)
            yg = jnp.concatenate(ys, axis=1)
            yg = yg * _silu(proj_ref[0, rows, OFF_Z + g * GROUP_CH:OFF_Z + (g + 1) * GROUP_CH])
            mix[rows, g * GROUP_CH:(g + 1) * GROUP_CH] = _rms(
                yg, gnw_ref[:, g * GROUP_CH:(g + 1) * GROUP_CH]).astype(BF16)
        return carry

    lax.fori_loop(0, tc // CHUNK, chunk_body, 0)
    del lane2
    o_ref[0] = h_ref[0] + jnp.dot(mix[...], wout_ref[...], preferred_element_type=F32)


def ssd_mixer(proj, h, cw, cb, dtb, aneg, dsk, gnw, scw, wout_bf16):
    b, t, _ = h.shape
    tc = ROW_TILE
    const = lambda shape: pl.BlockSpec(shape, lambda i, j: (0,) * len(shape))
    return pl.pallas_call(
        _ssd_body,
        grid=(b, t // tc),
        in_specs=[pl.BlockSpec((1, tc, EVEN_PROJ), lambda i, j: (i, j, 0)),
                  pl.BlockSpec((1, tc, D_MODEL), lambda i, j: (i, j, 0)),
                  const((SSM_CONV, SSM_CONV_CH)), const((1, SSM_CONV_CH)),
                  const((1, LANES)), const((1, LANES)), const((1, SSM_INNER)), const((1, SSM_INNER)),
                  const((SCONV_WIDTH, SCONV_DIM)), const((EVEN_MIX, D_MODEL))],
        out_specs=pl.BlockSpec((1, tc, D_MODEL), lambda i, j: (i, j, 0)),
        out_shape=jax.ShapeDtypeStruct((b, t, D_MODEL), F32),
        scratch_shapes=[pltpu.VMEM((tc + SUBLANES, SSM_CONV_CH), F32),
                        pltpu.VMEM((tc, SSM_CONV_CH), F32),
                        pltpu.VMEM((tc, EVEN_MIX), BF16),
                        pltpu.VMEM((SSM_PAIRS, LANES, SSM_STATE), F32),
                        pltpu.VMEM((tc + SUBLANES, SCONV_DIM), F32)],
        compiler_params=_params("parallel", "arbitrary"),
        name="ssd_mixer",
    )(proj, h, cw, cb, dtb, aneg, dsk, gnw, scw, wout_bf16)


def even_layer(h, nw, w_in, conv_w, conv_b, dt_bias, a_log, d_skip, gnorm_w, sconv_w, w_out):
    b, t, d = h.shape
    o1 = SSM_INNER
    o2 = o1 + SSM_CONV_CH
    o3 = o2 + SSM_HEADS
    o4 = o3 + SCONV_DIM
    o5 = o4 + SCONV_DIM
    pad = jnp.zeros((d, LANES - SSM_HEADS), F32)
    w_re = jnp.concatenate([w_in[:, :o2], w_in[:, o3:], w_in[:, o2:o3], pad], axis=1).astype(BF16)
    proj = norm_matmul(h.reshape(b * t, d), nw, w_re, 640).reshape(b, t, EVEN_PROJ)
    lane_pad = lambda v: jnp.concatenate([v.astype(F32), jnp.zeros((LANES - SSM_HEADS,), F32)]).reshape(1, LANES)
    aneg = lane_pad(-jnp.exp(a_log.astype(F32)))
    dsk = jnp.repeat(d_skip.astype(F32), SSM_HEAD_DIM).reshape(1, SSM_INNER)
    del o4, o5
    return ssd_mixer(proj, h, conv_w, conv_b.reshape(1, -1), lane_pad(dt_bias), aneg, dsk,
                     gnorm_w.reshape(1, -1), sconv_w, w_out.astype(BF16))


def _qkv_body(x_ref, nw_ref, w_ref, cos_ref, sa_ref, sb_ref, q_ref, k_ref, v_ref):
    xn = _rms(x_ref[...], nw_ref[...]).astype(BF16)
    cos, sa, sb = cos_ref[...], sa_ref[...], sb_ref[...]
    for c in range(ATT_DIM // LANES):
        for which, ref, scale in ((0, q_ref, ATT_HEAD_DIM ** -0.5), (1, k_ref, 1.0)):
            col = which * ATT_DIM + c * LANES
            a = jnp.dot(xn, w_ref[:, col:col + LANES], preferred_element_type=F32)
            r = a * cos + pltpu.roll(a, ROPE_DIM // 2, 1) * sa + pltpu.roll(a, LANES - ROPE_DIM // 2, 1) * sb
            ref[:, c * LANES:(c + 1) * LANES] = (r * scale).astype(BF16)
        col = 2 * ATT_DIM + c * LANES
        v_ref[:, c * LANES:(c + 1) * LANES] = jnp.dot(
            xn, w_ref[:, col:col + LANES], preferred_element_type=F32).astype(BF16)


def _rope_tables(t):
    half = ROPE_DIM // 2
    inv = 1.0 / (ROPE_THETA ** (jnp.arange(0, ROPE_DIM, 2, dtype=F32) / ROPE_DIM))
    ang = jnp.arange(t, dtype=F32)[:, None] * inv[None, :]
    cos, sin = jnp.cos(ang), jnp.sin(ang)
    rest = ATT_HEAD_DIM - ROPE_DIM
    one = jnp.ones((t, rest), F32)
    zero = jnp.zeros((t, rest), F32)
    zh = jnp.zeros((t, half), F32)
    cos64 = jnp.concatenate([cos, cos, one], axis=1)
    sa64 = jnp.concatenate([zh, sin, zero], axis=1)
    sb64 = jnp.concatenate([-sin, zh, zero], axis=1)
    tile2 = lambda v: jnp.concatenate([v, v], axis=1)
    return tile2(cos64), tile2(sa64), tile2(sb64)


def qkv_rope(x, nw, w_bf16, tables, t):
    n, d = x.shape
    tiles_per_seq = t // ROW_TILE
    tab = pl.BlockSpec((ROW_TILE, LANES), lambda i: (i % tiles_per_seq, 0))
    out = pl.BlockSpec((ROW_TILE, ATT_DIM), lambda i: (i, 0))
    return pl.pallas_call(
        _qkv_body,
        grid=(n // ROW_TILE,),
        in_specs=[pl.BlockSpec((ROW_TILE, d), lambda i: (i, 0)),
                  pl.BlockSpec((1, d), lambda i: (0, 0)),
                  pl.BlockSpec((d, 3 * ATT_DIM), lambda i: (0, 0)),
                  tab, tab, tab],
        out_specs=[out, out, out],
        out_shape=[jax.ShapeDtypeStruct((n, ATT_DIM), BF16)] * 3,
        compiler_params=_params("parallel"),
        name="qkv_rope",
    )(x, nw.reshape(1, d), w_bf16, *tables)


def _attn_body(q_ref, k_ref, v_ref, lam_ref, sw_ref, o_ref, m_scr, l_scr, acc_scr, *, lambda_init):
    tq = ATT_TILE
    qi = pl.program_id(2)
    q = q_ref[0]
    lane = lax.broadcasted_iota(jnp.int32, (tq, LANES), 1)
    zero = jnp.zeros_like(q)
    qs = jnp.concatenate([jnp.where(lane < ATT_HEAD_DIM, q, zero), jnp.where(lane < ATT_HEAD_DIM, zero, q)], axis=0)

    def scores(j):
        off = pl.multiple_of(j * tq, tq)
        kc = k_ref[0, pl.ds(off, tq), :]
        vc = v_ref[0, pl.ds(off, tq), :]
        return lax.dot_general(qs, kc, (((1,), (1,)), ((), ())), preferred_element_type=F32), vc

    s, vc = scores(qi)
    rq = lax.broadcasted_iota(jnp.int32, (2 * tq, tq), 0)
    rk = lax.broadcasted_iota(jnp.int32, (2 * tq, tq), 1)
    rq = jnp.where(rq >= tq, rq - tq, rq)
    s = jnp.where((rk // CHUNK) <= (rq // CHUNK), s, -jnp.inf)
    m0 = jnp.max(s, axis=-1, keepdims=True)
    p = jnp.exp(s - m0)
    m_scr[...] = m0
    l_scr[...] = jnp.sum(p, axis=-1, keepdims=True)
    acc_scr[...] = jnp.dot(p.astype(BF16), vc, preferred_element_type=F32)

    def body(j, carry):
        s, vc = scores(j)
        m_old = m_scr[...]
        m_new = jnp.maximum(m_old, jnp.max(s, axis=-1, keepdims=True))
        alpha = jnp.exp(m_old - m_new)
        p = jnp.exp(s - m_new)
        m_scr[...] = m_new
        l_scr[...] = alpha * l_scr[...] + jnp.sum(p, axis=-1, keepdims=True)
        acc_scr[...] = alpha * acc_scr[...] + jnp.dot(p.astype(BF16), vc, preferred_element_type=F32)
        return carry

    lax.fori_loop(0, qi, body, 0)

    lp = lam_ref[...]
    lam = (jnp.exp(jnp.sum(lp[0:1] * lp[1:2], axis=-1, keepdims=True))
           - jnp.exp(jnp.sum(lp[2:3] * lp[3:4], axis=-1, keepdims=True)) + lambda_init)
    o = acc_scr[0:tq, :] / l_scr[0:tq, :] - lam * (acc_scr[tq:2 * tq, :] / l_scr[tq:2 * tq, :])
    o_ref[0] = (_rms(o, sw_ref[...]) * (1.0 - lambda_init)).astype(o_ref.dtype)


def diff_attention(q, k, v, lam_p, subln_w, lambda_init):
    b, t, _ = q.shape
    tq = ATT_TILE
    kv = pl.BlockSpec((1, t, LANES), lambda i, h, j: (i, 0, h))
    qo = pl.BlockSpec((1, tq, LANES), lambda i, h, j: (i, j, h))
    return pl.pallas_call(
        functools.partial(_attn_body, lambda_init=lambda_init),
        grid=(b, ATT_HEADS, t // tq),
        in_specs=[qo, kv, kv,
                  pl.BlockSpec((4, ATT_HEAD_DIM), lambda i, h, j: (0, 0)),
                  pl.BlockSpec((1, LANES), lambda i, h, j: (0, 0))],
        out_specs=qo,
        out_shape=jax.ShapeDtypeStruct((b, t, ATT_DIM), BF16),
        scratch_shapes=[pltpu.VMEM((2 * tq, 1), F32), pltpu.VMEM((2 * tq, 1), F32),
                        pltpu.VMEM((2 * tq, LANES), F32)],
        compiler_params=_params("parallel", "parallel", "arbitrary"),
        name="diff_attention",
    )(q, k, v, lam_p, subln_w.reshape(1, LANES))


def _matmul_res_body(a_ref, w_ref, h_ref, o_ref):
    o_ref[...] = h_ref[...] + jnp.dot(a_ref[...], w_ref[...], preferred_element_type=F32)


def matmul_residual(a_bf16, w_bf16, h):
    n, kdim = a_bf16.shape
    d = w_bf16.shape[1]
    return pl.pallas_call(
        _matmul_res_body,
        grid=(n // ROW_TILE,),
        in_specs=[pl.BlockSpec((ROW_TILE, kdim), lambda i: (i, 0)),
                  pl.BlockSpec((kdim, d), lambda i: (0, 0)),
                  pl.BlockSpec((ROW_TILE, d), lambda i: (i, 0))],
        out_specs=pl.BlockSpec((ROW_TILE, d), lambda i: (i, 0)),
        out_shape=jax.ShapeDtypeStruct((n, d), F32),
        compiler_params=_params("parallel"),
        name="matmul_residual",
    )(a_bf16, w_bf16, h)


def odd_layer(h, nw, w_qkv, lam_p, subln_w, w_out, lambda_init, tables):
    b, t, d = h.shape
    q, k, v = qkv_rope(h.reshape(b * t, d), nw, w_qkv.astype(BF16), tables, t)
    shape = (b, t, ATT_DIM)
    o = diff_attention(q.reshape(shape), k.reshape(shape), v.reshape(shape), lam_p.astype(F32),
                       subln_w.astype(F32), lambda_init)
    return matmul_residual(o.reshape(b * t, ATT_DIM), w_out.astype(BF16), h.reshape(b * t, d)).reshape(b, t, d)


META_E, META_R, META_G = 0, 2, 4


def _router_body(h_ref, nw_ref, wr_ref, br_ref, meta_ref, cnt_ref, run_cnt):
    tm = MOE_TILE

    @pl.when(pl.program_id(0) == 0)
    def _():
        run_cnt[...] = jnp.zeros_like(run_cnt)

    xn = _rms(h_ref[...], nw_ref[...])
    logits = jnp.dot(xn, wr_ref[...], preferred_element_type=F32, precision=lax.Precision.HIGHEST) + br_ref[...]
    lane = lax.broadcasted_iota(jnp.int32, (tm, LANES), 1)
    big = jnp.int32(LANES)
    neg = -jnp.inf

    is_g = lane < MOE_GROUPS
    gl = jnp.where(is_g, logits, neg)
    gexp = jnp.exp(gl - jnp.max(gl, axis=-1, keepdims=True))
    gprob = gexp / jnp.sum(gexp, axis=-1, keepdims=True)
    g_gate = jnp.max(gprob, axis=-1, keepdims=True)
    g_idx = jnp.min(jnp.where(is_g & (gprob == g_gate), lane, big), axis=-1, keepdims=True)

    lo = ROUTER_LANE0 + g_idx * EXPERTS_PER_GROUP
    sel = (lane >= lo) & (lane < lo + EXPERTS_PER_GROUP)
    el = jnp.where(sel, logits, neg)
    eexp = jnp.exp(el - jnp.max(el, axis=-1, keepdims=True))
    eprob = jnp.where(sel, eexp / jnp.sum(eexp, axis=-1, keepdims=True), -1.0)
    p1 = jnp.max(eprob, axis=-1, keepdims=True)
    i1 = jnp.min(jnp.where(eprob == p1, lane, big), axis=-1, keepdims=True)
    rest = jnp.where(lane == i1, -1.0, eprob)
    p2 = jnp.max(rest, axis=-1, keepdims=True)
    i2 = jnp.min(jnp.where(rest == p2, lane, big), axis=-1, keepdims=True)
    g1 = g_gate * p1 / (p1 + p2)
    g2 = g_gate * p2 / (p1 + p2)

    hit1 = lane == i1
    hit2 = lane == i2
    assign = jnp.where(hit1 | hit2, 1.0, 0.0)
    rr = lax.broadcasted_iota(jnp.int32, (tm, tm), 0)
    cc = lax.broadcasted_iota(jnp.int32, (tm, tm), 1)
    before = jnp.where(cc < rr, 1.0, 0.0).astype(BF16)
    prior = jnp.dot(before, assign.astype(BF16), preferred_element_type=F32) + run_cnt[...]
    r1 = jnp.sum(jnp.where(hit1, prior, 0.0), axis=-1, keepdims=True)
    r2 = jnp.sum(jnp.where(hit2, prior, 0.0), axis=-1, keepdims=True)
    run_cnt[...] = run_cnt[...] + jnp.sum(assign, axis=0, keepdims=True)
    cnt_ref[...] = jnp.broadcast_to(run_cnt[...], cnt_ref.shape)

    e1 = (i1 - ROUTER_LANE0).astype(F32)
    e2 = (i2 - ROUTER_LANE0).astype(F32)
    rec = jnp.zeros((tm, LANES), F32)
    for pos, val in ((META_E, e1), (META_E + 1, e2), (META_R, r1), (META_R + 1, r2), (META_G, g1), (META_G + 1, g2)):
        rec = jnp.where(lane == pos, val, rec)
    meta_ref[...] = rec


def moe_router(h, nw, w_router, b_router):
    n, d = h.shape
    tm = MOE_TILE
    return pl.pallas_call(
        _router_body,
        grid=(n // tm,),
        in_specs=[pl.BlockSpec((tm, d), lambda i: (i, 0)),
                  pl.BlockSpec((1, d), lambda i: (0, 0)),
                  pl.BlockSpec((d, LANES), lambda i: (0, 0)),
                  pl.BlockSpec((1, LANES), lambda i: (0, 0))],
        out_specs=[pl.BlockSpec((tm, LANES), lambda i: (i, 0)),
                   pl.BlockSpec((SUBLANES, LANES), lambda i: (0, 0))],
        out_shape=[jax.ShapeDtypeStruct((n, LANES), F32), jax.ShapeDtypeStruct((SUBLANES, LANES), F32)],
        scratch_shapes=[pltpu.VMEM((1, LANES), F32)],
        compiler_params=_params("arbitrary"),
        name="moe_router",
    )(h, nw.reshape(1, d), w_router, b_router)


def _row_copy(src, src_row, dst, dst_row, sem):
    return pltpu.make_async_copy(src.at[pl.ds(src_row, 1), :], dst.at[pl.ds(dst_row, 1), :], sem)


def _dispatch_body(dest_ref, h_ref, nw_ref, zeros_ref, xrows_ref, xn_scr, sem):
    del zeros_ref
    tm = MOE_TILE
    xn_scr[...] = _rms(h_ref[...], nw_ref[...])

    def issue(i, carry):
        _row_copy(xn_scr, i, xrows_ref, dest_ref[0, 0, 2 * i], sem).start()
        _row_copy(xn_scr, i, xrows_ref, dest_ref[0, 0, 2 * i + 1], sem).start()
        return carry

    lax.fori_loop(0, tm, issue, 0)

    def drain(i, carry):
        _row_copy(xn_scr, 0, xrows_ref, 0, sem).wait()
        return carry

    lax.fori_loop(0, 2 * tm, drain, 0)


def moe_dispatch(h, nw, dest, n_rows):
    n, d = h.shape
    tm = MOE_TILE
    zeros = jnp.zeros((n_rows, d), F32)
    return pl.pallas_call(
        _dispatch_body,
        grid=(n // tm,),
        in_specs=[pl.BlockSpec((1, 1, 2 * tm), lambda i: (i, 0, 0), memory_space=pltpu.SMEM),
                  pl.BlockSpec((tm, d), lambda i: (i, 0)),
                  pl.BlockSpec((1, d), lambda i: (0, 0)),
                  pl.BlockSpec(memory_space=pl.ANY)],
        out_specs=pl.BlockSpec(memory_space=pl.ANY),
        out_shape=jax.ShapeDtypeStruct((n_rows, d), F32),
        scratch_shapes=[pltpu.VMEM((tm, d), F32), pltpu.SemaphoreType.DMA],
        input_output_aliases={3: 0},
        compiler_params=_params("arbitrary"),
        name="moe_dispatch",
    )(dest.reshape(n // tm, 1, 2 * tm), h, nw.reshape(1, d), zeros)


def _ffn_body(blk_e_ref, n_used_ref, x_ref, w1_ref, w3_ref, w2_ref, y_ref, w1b, w3b, w2b):
    b = pl.program_id(0)
    prev = blk_e_ref[jnp.maximum(b - 1, 0)]
    active = b < n_used_ref[0]

    @pl.when(active & ((b == 0) | (blk_e_ref[b] != prev)))
    def _():
        w1b[...] = w1_ref[0].astype(BF16)
        w3b[...] = w3_ref[0].astype(BF16)
        w2b[...] = w2_ref[0].astype(BF16)

    @pl.when(active)
    def _():
        x = x_ref[...].astype(BF16)
        u = jnp.dot(x, w1b[...], preferred_element_type=F32)
        g = jnp.dot(x, w3b[...], preferred_element_type=F32)
        y_ref[...] = jnp.dot((_silu(u) * g).astype(BF16), w2b[...], preferred_element_type=F32)

    @pl.when(jnp.logical_not(active))
    def _():
        y_ref[...] = jnp.zeros_like(y_ref)


def moe_ffn(x_rows, blk_expert, n_used, w1, w3, w2):
    n_rows, d = x_rows.shape
    de = w1.shape[-1]
    n_blk = n_rows // MOE_BLOCK
    row = lambda b, be, nu: (b, 0)
    wsel = lambda b, be, nu: (be[b], 0, 0)
    return pl.pallas_call(
        _ffn_body,
        grid_spec=pltpu.PrefetchScalarGridSpec(
            num_scalar_prefetch=2,
            grid=(n_blk,),
            in_specs=[pl.BlockSpec((MOE_BLOCK, d), row),
                      pl.BlockSpec((1, d, de), wsel),
                      pl.BlockSpec((1, d, de), wsel),
                      pl.BlockSpec((1, de, d), wsel)],
            out_specs=pl.BlockSpec((MOE_BLOCK, d), row),
            scratch_shapes=[pltpu.VMEM((d, de), BF16), pltpu.VMEM((d, de), BF16), pltpu.VMEM((de, d), BF16)]),
        out_shape=jax.ShapeDtypeStruct((n_rows, d), F32),
        compiler_params=_params("arbitrary"),
        name="moe_ffn",
    )(blk_expert, n_used, x_rows, w1, w3, w2)


def _combine_body(dest_ref, meta_ref, h_ref, yrows_ref, fw_ref, o_ref, ybuf, sem, *, final_norm):
    tm = MOE_TILE

    def issue(i, carry):
        _row_copy(yrows_ref, dest_ref[0, 0, 2 * i], ybuf.at[0], i, sem).start()
        _row_copy(yrows_ref, dest_ref[0, 0, 2 * i + 1], ybuf.at[1], i, sem).start()
        return carry

    lax.fori_loop(0, tm, issue, 0)

    def drain(i, carry):
        _row_copy(yrows_ref, 0, ybuf.at[0], 0, sem).wait()
        return carry

    lax.fori_loop(0, 2 * tm, drain, 0)

    meta = meta_ref[...]
    out = h_ref[...] + (meta[:, META_G:META_G + 1] * ybuf[0] + meta[:, META_G + 1:META_G + 2] * ybuf[1])
    if final_norm:
        out = _rms(out, fw_ref[...])
    o_ref[...] = out


def moe_combine(h, meta, dest, y_rows, final_w):
    n, d = h.shape
    tm = MOE_TILE
    final_norm = final_w is not None
    fw = (final_w if final_norm else jnp.ones((d,), F32)).reshape(1, d)
    return pl.pallas_call(
        functools.partial(_combine_body, final_norm=final_norm),
        grid=(n // tm,),
        in_specs=[pl.BlockSpec((1, 1, 2 * tm), lambda i: (i, 0, 0), memory_space=pltpu.SMEM),
                  pl.BlockSpec((tm, LANES), lambda i: (i, 0)),
                  pl.BlockSpec((tm, d), lambda i: (i, 0)),
                  pl.BlockSpec(memory_space=pl.ANY),
                  pl.BlockSpec((1, d), lambda i: (0, 0))],
        out_specs=pl.BlockSpec((tm, d), lambda i: (i, 0)),
        out_shape=jax.ShapeDtypeStruct((n, d), F32),
        scratch_shapes=[pltpu.VMEM((2, tm, d), F32), pltpu.SemaphoreType.DMA],
        compiler_params=_params("arbitrary"),
        name="moe_combine",
    )(dest.reshape(n // tm, 1, 2 * tm), meta, h, y_rows, fw)


def hier_moe_layer(h, nw, w_group, b_group, w_expert, b_expert, w1, w3, w2, final_w):
    n, d = h.shape
    pad_cols = LANES - MOE_GROUPS - N_EXPERTS
    w_router = jnp.concatenate([w_group, w_expert, jnp.zeros((d, pad_cols), F32)], axis=1)
    b_router = jnp.concatenate([b_group, b_expert, jnp.zeros((pad_cols,), F32)]).reshape(1, LANES)
    meta, cnt = moe_router(h, nw, w_router, b_router)

    counts = cnt[0, ROUTER_LANE0:ROUTER_LANE0 + N_EXPERTS].astype(jnp.int32)
    padded = (counts + MOE_BLOCK - 1) // MOE_BLOCK * MOE_BLOCK
    pends = jnp.cumsum(padded)
    pstarts = pends - padded
    expert = meta[:, META_E:META_E + 2].astype(jnp.int32)
    rank = meta[:, META_R:META_R + 2].astype(jnp.int32)
    dest = (pstarts[expert] + rank).reshape(n * 2)
    n_rows = n * 2 + N_EXPERTS * MOE_BLOCK
    n_blk = n_rows // MOE_BLOCK
    blk_expert = jnp.minimum(
        jnp.searchsorted(pends, jnp.arange(n_blk, dtype=jnp.int32) * MOE_BLOCK, side="right"),
        N_EXPERTS - 1).astype(jnp.int32)
    n_used = (pends[-1:] // MOE_BLOCK).astype(jnp.int32)

    x_rows = moe_dispatch(h, nw, dest, n_rows)
    y_rows = moe_ffn(x_rows, blk_expert, n_used, w1, w3, w2)
    return moe_combine(h, meta, dest, y_rows, final_w)


def _lambda_init(layer):
    return 0.8 - 0.6 * math.exp(-0.3 * layer)


def kernel(x, norm_mix_w, norm_ffn_w, final_norm_w, ev_w_in, ev_conv_w, ev_conv_b, ev_dt_bias, ev_a_log,
           ev_d_skip, ev_gnorm_w, ev_sconv_w, ev_w_out, od_w_qkv, od_lambda, od_subln_w, od_w_out,
           moe_w_group, moe_b_group, moe_w_expert, moe_b_expert, moe_w1, moe_w3, moe_w2):
    b, t, d = x.shape
    depth = norm_mix_w.shape[0]
    tables = _rope_tables(t)
    h = x
    for layer in range(depth):
        i = layer // 2
        if layer % 2 == 0:
            h = even_layer(h, norm_mix_w[layer], ev_w_in[i], ev_conv_w[i], ev_conv_b[i], ev_dt_bias[i],
                           ev_a_log[i], ev_d_skip[i], ev_gnorm_w[i], ev_sconv_w[i], ev_w_out[i])
        else:
            h = odd_layer(h, norm_mix_w[layer], od_w_qkv[i], od_lambda[i], od_subln_w[i], od_w_out[i],
                          _lambda_init(layer), tables)
        final_w = final_norm_w if layer == depth - 1 else None
        h = hier_moe_layer(h.reshape(b * t, d), norm_ffn_w[layer], moe_w_group[layer], moe_b_group[layer],
                           moe_w_expert[layer], moe_b_expert[layer], moe_w1[layer], moe_w3[layer],
                           moe_w2[layer], final_w).reshape(b, t, d)
    return h
```

```python
import functools
import math

import jax
import jax.numpy as jnp
from jax import lax
from jax.experimental import pallas as pl
from jax.experimental.pallas import tpu as pltpu

F32 = jnp.float32
BF16 = jnp.bfloat16

D_MODEL = 1024
CHUNK = 64
SSM_INNER = 1536
SSM_HEAD_DIM = 64
SSM_HEADS = 24
SSM_GROUPS = 4
SSM_STATE = 128
SSM_CONV = 4
SSM_CONV_CH = SSM_INNER + 2 * SSM_GROUPS * SSM_STATE
SSM_PAIRS = SSM_HEADS // 2
PAIRS_PER_GROUP = SSM_PAIRS // SSM_GROUPS
GROUP_CH = SSM_INNER // SSM_GROUPS
SCONV_DIM = 512
SCONV_WIDTH = 3
EVEN_MIX = SSM_INNER + SCONV_DIM
ATT_HEADS = 8
ATT_HEAD_DIM = 64
ATT_DIM = ATT_HEADS * 2 * ATT_HEAD_DIM
ROPE_DIM = ATT_HEAD_DIM // 4
ROPE_THETA = 500000.0
MOE_GROUPS = 8
EXPERTS_PER_GROUP = 8
N_EXPERTS = 64
D_EXPERT = 512
EPS = 1e-6

LANES = 128
SUBLANES = 8
VMEM_LIMIT = 56 * 1024 * 1024

OFF_Z = 0
OFF_XBC = SSM_INNER
OFF_HB = OFF_XBC + SSM_CONV_CH
OFF_BG = OFF_HB + SCONV_DIM
OFF_CG = OFF_BG + SCONV_DIM
OFF_DT = OFF_CG + SCONV_DIM
EVEN_PROJ = OFF_DT + LANES

ROW_TILE = 256
ATT_TILE = 256
ATT_GROUP = 4
VT_ROWS = LANES + 16
LOG2E = 1.4426950408889634
MOE_TILE = 512
MOE_BLOCK = 256
DMA_UNROLL = 8
ROUTER_LANE0 = MOE_GROUPS


def _params(*sem):
    return pltpu.CompilerParams(dimension_semantics=sem, vmem_limit_bytes=VMEM_LIMIT)


def _rms(x, w):
    return x * lax.rsqrt(jnp.mean(x * x, axis=-1, keepdims=True) + EPS) * w


def _silu(x):
    return x * (1.0 / (1.0 + jnp.exp(-x)))


def _norm_matmul_body(x_ref, nw_ref, w_ref, o_ref, *, col_chunk):
    xn = _rms(x_ref[...], nw_ref[...]).astype(BF16)
    for c0 in range(0, o_ref.shape[-1], col_chunk):
        o_ref[:, c0:c0 + col_chunk] = jnp.dot(
            xn, w_ref[:, c0:c0 + col_chunk], preferred_element_type=F32).astype(o_ref.dtype)


def norm_matmul(x, nw, w_bf16, col_chunk):
    n, d = x.shape
    f = w_bf16.shape[1]
    return pl.pallas_call(
        functools.partial(_norm_matmul_body, col_chunk=col_chunk),
        grid=(n // ROW_TILE,),
        in_specs=[pl.BlockSpec((ROW_TILE, d), lambda i: (i, 0)),
                  pl.BlockSpec((1, d), lambda i: (0, 0)),
                  pl.BlockSpec((d, f), lambda i: (0, 0))],
        out_specs=pl.BlockSpec((ROW_TILE, f), lambda i: (i, 0)),
        out_shape=jax.ShapeDtypeStruct((n, f), F32),
        compiler_params=_params("parallel"),
        name="norm_matmul",
    )(x, nw.reshape(1, d), w_bf16)


def _softplus(x):
    return jnp.maximum(x, 0.0) + jnp.log(1.0 + jnp.exp(-jnp.abs(x)))


def _ssd_body(proj_ref, h_ref, cw_ref, cb_ref, dtb_ref, aneg_ref, dsk_ref, gnw_ref, scw_ref, wout_ref,
              o_ref, cbuf, xact, mix, state, sbuf):
    tc = ROW_TILE
    halo = SUBLANES

    @pl.when(pl.program_id(1) == 0)
    def _():
        state[...] = jnp.zeros_like(state)
        cbuf[0:halo, :] = jnp.zeros((halo, SSM_CONV_CH), F32)
        sbuf[0:halo, :] = jnp.zeros((halo, SCONV_DIM), F32)

    cbuf[halo:halo + tc, :] = proj_ref[0, :, OFF_XBC:OFF_XBC + SSM_CONV_CH]
    for c0 in range(0, SSM_CONV_CH, 512):
        xin = cbuf[:, c0:c0 + 512]
        acc = cb_ref[:, c0:c0 + 512] + cw_ref[SSM_CONV - 1:SSM_CONV, c0:c0 + 512] * xin
        for back in range(1, SSM_CONV):
            j = SSM_CONV - 1 - back
            acc = acc + cw_ref[j:j + 1, c0:c0 + 512] * pltpu.roll(xin, back, 0)
        xact[:, c0:c0 + 512] = _silu(acc[halo:halo + tc])
    cbuf[0:halo, :] = cbuf[tc:tc + halo, :]

    sbuf[halo:halo + tc, :] = (proj_ref[0, :, OFF_CG:OFF_CG + SCONV_DIM]
                               * proj_ref[0, :, OFF_HB:OFF_HB + SCONV_DIM])
    sin = sbuf[...]
    conv = scw_ref[SCONV_WIDTH - 1:SCONV_WIDTH, :] * sin
    for back in range(1, SCONV_WIDTH):
        j = SCONV_WIDTH - 1 - back
        conv = conv + scw_ref[j:j + 1, :] * pltpu.roll(sin, back, 0)
    mix[:, SSM_INNER:EVEN_MIX] = (proj_ref[0, :, OFF_BG:OFF_BG + SCONV_DIM] * conv[halo:halo + tc]).astype(BF16)
    sbuf[0:halo, :] = sbuf[tc:tc + halo, :]

    sub = lax.broadcasted_iota(jnp.int32, (CHUNK, LANES), 0)
    lane = lax.broadcasted_iota(jnp.int32, (CHUNK, LANES), 1)
    lane_lo = lane < SSM_HEAD_DIM
    tril2 = sub >= jnp.where(lane_lo, lane, lane - SSM_HEAD_DIM)
    csum_l = lax.broadcasted_iota(jnp.int32, (CHUNK, CHUNK), 0)
    csum_s = lax.broadcasted_iota(jnp.int32, (CHUNK, CHUNK), 1)
    tril_incl = (csum_s <= csum_l).astype(F32)
    sub2 = lax.broadcasted_iota(jnp.int32, (LANES, LANES), 0)
    lane2 = lax.broadcasted_iota(jnp.int32, (LANES, LANES), 1)

    def chunk_body(c, carry):
        r0 = pl.multiple_of(c * CHUNK, CHUNK)
        rows = pl.ds(r0, CHUNK)
        dt = _softplus(proj_ref[0, rows, OFF_DT:OFF_DT + LANES] + dtb_ref[...])
        a = dt * aneg_ref[...]
        acum = jnp.dot(tril_incl, a, preferred_element_type=F32, precision=lax.Precision.HIGHEST)
        acum_t = acum.T
        for g in range(SSM_GROUPS):
            bg = xact[rows, SSM_INNER + g * SSM_STATE:SSM_INNER + (g + 1) * SSM_STATE].astype(BF16)
            c_off = SSM_INNER + SSM_GROUPS * SSM_STATE
            cg = xact[rows, c_off + g * SSM_STATE:c_off + (g + 1) * SSM_STATE].astype(BF16)
            cb = lax.dot_general(cg, bg, (((1,), (1,)), ((), ())), preferred_element_type=F32)
            cb2 = jnp.concatenate([cb, cb], axis=1)
            ys = []
            for jj in range(PAIRS_PER_GROUP):
                j = g * PAIRS_PER_GROUP + jj
                h0, h1 = 2 * j, 2 * j + 1
                xp = xact[rows, j * LANES:(j + 1) * LANES]
                dt2 = jnp.where(lane_lo, dt[:, h0:h0 + 1], dt[:, h1:h1 + 1])
                col2 = jnp.where(lane_lo, acum[:, h0:h0 + 1], acum[:, h1:h1 + 1])
                row2 = jnp.concatenate([acum_t[h0:h0 + 1, :], acum_t[h1:h1 + 1, :]], axis=1)
                last2 = col2[CHUNK - 1:CHUNK, :]
                xdt = xp * dt2
                m2 = cb2 * jnp.exp(jnp.where(tril2, col2 - row2, -jnp.inf))
                rhs = jnp.concatenate([jnp.where(lane_lo, xdt, 0.0), jnp.where(lane_lo, 0.0, xdt)],
                                      axis=0).astype(BF16)
                y2 = jnp.dot(m2.astype(BF16), rhs, preferred_element_type=F32)
                s_old = state[j]
                y_in = lax.dot_general(cg, s_old.astype(BF16), (((1,), (1,)), ((), ())),
                                       preferred_element_type=F32)
                y2 = y2 + y_in * jnp.exp(col2) + dsk_ref[:, j * LANES:(j + 1) * LANES] * xp
                xd = (xdt * jnp.exp(last2 - col2)).astype(BF16)
                upd = lax.dot_general(xd, bg, (((0,), (0,)), ((), ())), preferred_element_type=F32)
                e0 = jnp.exp(acum_t[h0:h0 + 1, CHUNK - 1:CHUNK])
                e1 = jnp.exp(acum_t[h1:h1 + 1, CHUNK - 1:CHUNK])
                state[j] = s_old * jnp.where(sub2 < SSM_HEAD_DIM, e0, e1) + upd
                ys.append(y2)
            yg = jnp.concatenate(ys, axis=1)
            yg = yg * _silu(proj_ref[0, rows, OFF_Z + g * GROUP_CH:OFF_Z + (g + 1) * GROUP_CH])
            mix[rows, g * GROUP_CH:(g + 1) * GROUP_CH] = _rms(
                yg, gnw_ref[:, g * GROUP_CH:(g + 1) * GROUP_CH]).astype(BF16)
        return carry

    lax.fori_loop(0, tc // CHUNK, chunk_body, 0)
    del lane2
    o_ref[0] = h_ref[0] + jnp.dot(mix[...], wout_ref[...], preferred_element_type=F32)


def ssd_mixer(proj, h, cw, cb, dtb, aneg, dsk, gnw, scw, wout_bf16):
    b, t, _ = h.shape
    tc = ROW_TILE
    const = lambda shape: pl.BlockSpec(shape, lambda i, j: (0,) * len(shape))
    return pl.pallas_call(
        _ssd_body,
        grid=(b, t // tc),
        in_specs=[pl.BlockSpec((1, tc, EVEN_PROJ), lambda i, j: (i, j, 0)),
                  pl.BlockSpec((1, tc, D_MODEL), lambda i, j: (i, j, 0)),
                  const((SSM_CONV, SSM_CONV_CH)), const((1, SSM_CONV_CH)),
                  const((1, LANES)), const((1, LANES)), const((1, SSM_INNER)), const((1, SSM_INNER)),
                  const((SCONV_WIDTH, SCONV_DIM)), const((EVEN_MIX, D_MODEL))],
        out_specs=pl.BlockSpec((1, tc, D_MODEL), lambda i, j: (i, j, 0)),
        out_shape=jax.ShapeDtypeStruct((b, t, D_MODEL), F32),
        scratch_shapes=[pltpu.VMEM((tc + SUBLANES, SSM_CONV_CH), F32),
                        pltpu.VMEM((tc, SSM_CONV_CH), F32),
                        pltpu.VMEM((tc, EVEN_MIX), BF16),
                        pltpu.VMEM((SSM_PAIRS, LANES, SSM_STATE), F32),
                        pltpu.VMEM((tc + SUBLANES, SCONV_DIM), F32)],
        compiler_params=_params("parallel", "arbitrary"),
        name="ssd_mixer",
    )(proj, h, cw, cb, dtb, aneg, dsk, gnw, scw, wout_bf16)


def even_layer(h, nw, w_in, conv_w, conv_b, dt_bias, a_log, d_skip, gnorm_w, sconv_w, w_out):
    b, t, d = h.shape
    o1 = SSM_INNER
    o2 = o1 + SSM_CONV_CH
    o3 = o2 + SSM_HEADS
    o4 = o3 + SCONV_DIM
    o5 = o4 + SCONV_DIM
    pad = jnp.zeros((d, LANES - SSM_HEADS), F32)
    w_re = jnp.concatenate([w_in[:, :o2], w_in[:, o3:], w_in[:, o2:o3], pad], axis=1).astype(BF16)
    proj = norm_matmul(h.reshape(b * t, d), nw, w_re, 640).reshape(b, t, EVEN_PROJ)
    lane_pad = lambda v: jnp.concatenate([v.astype(F32), jnp.zeros((LANES - SSM_HEADS,), F32)]).reshape(1, LANES)
    aneg = lane_pad(-jnp.exp(a_log.astype(F32)))
    dsk = jnp.repeat(d_skip.astype(F32), SSM_HEAD_DIM).reshape(1, SSM_INNER)
    del o4, o5
    return ssd_mixer(proj, h, conv_w, conv_b.reshape(1, -1), lane_pad(dt_bias), aneg, dsk,
                     gnorm_w.reshape(1, -1), sconv_w, w_out.astype(BF16))


def _qkv_body(x_ref, nw_ref, w_ref, cos_ref, sa_ref, sb_ref, qt_ref, k_ref, vt_ref):
    xn = _rms(x_ref[...], nw_ref[...]).astype(BF16)
    cos, sa, sb = cos_ref[...], sa_ref[...], sb_ref[...]

    def rope(a):
        return a * cos + pltpu.roll(a, ROPE_DIM // 2, 1) * sa + pltpu.roll(a, LANES - ROPE_DIM // 2, 1) * sb

    for c in range(ATT_HEADS):
        proj = lambda part: jnp.dot(xn, w_ref[:, part * ATT_DIM + c * LANES:part * ATT_DIM + (c + 1) * LANES],
                                    preferred_element_type=F32)
        qt_ref[0, c, 0] = (rope(proj(0)) * (ATT_HEAD_DIM ** -0.5 * LOG2E)).astype(BF16).T
        k_ref[:, c * LANES:(c + 1) * LANES] = rope(proj(1)).astype(BF16)
        vt_ref[0, c, 0, 0:LANES, :] = proj(2).astype(BF16).T
        row = lax.broadcasted_iota(jnp.int32, (VT_ROWS - LANES, ROW_TILE), 0)
        vt_ref[0, c, 0, LANES:VT_ROWS, :] = jnp.where(row == 0, 1.0, 0.0).astype(BF16)


def _rope_tables(t):
    half = ROPE_DIM // 2
    inv = 1.0 / (ROPE_THETA ** (jnp.arange(0, ROPE_DIM, 2, dtype=F32) / ROPE_DIM))
    ang = jnp.arange(t, dtype=F32)[:, None] * inv[None, :]
    cos, sin = jnp.cos(ang), jnp.sin(ang)
    rest = ATT_HEAD_DIM - ROPE_DIM
    one = jnp.ones((t, rest), F32)
    zero = jnp.zeros((t, rest), F32)
    zh = jnp.zeros((t, half), F32)
    cos64 = jnp.concatenate([cos, cos, one], axis=1)
    sa64 = jnp.concatenate([zh, sin, zero], axis=1)
    sb64 = jnp.concatenate([-sin, zh, zero], axis=1)
    tile2 = lambda v: jnp.concatenate([v, v], axis=1)
    return tile2(cos64), tile2(sa64), tile2(sb64)


def qkv_rope(x, nw, w_bf16, tables, b, t):
    n, d = x.shape
    assert ROW_TILE == ATT_TILE
    tiles = t // ROW_TILE
    tab = pl.BlockSpec((ROW_TILE, LANES), lambda i: (i % tiles, 0))
    tr = lambda rows: pl.BlockSpec((1, ATT_HEADS, 1, rows, ROW_TILE), lambda i: (i // tiles, 0, i % tiles, 0, 0))
    tr_shape = lambda rows: jax.ShapeDtypeStruct((b, ATT_HEADS, tiles, rows, ROW_TILE), BF16)
    return pl.pallas_call(
        _qkv_body,
        grid=(n // ROW_TILE,),
        in_specs=[pl.BlockSpec((ROW_TILE, d), lambda i: (i, 0)),
                  pl.BlockSpec((1, d), lambda i: (0, 0)),
                  pl.BlockSpec((d, 3 * ATT_DIM), lambda i: (0, 0)),
                  tab, tab, tab],
        out_specs=[tr(LANES), pl.BlockSpec((ROW_TILE, ATT_DIM), lambda i: (i, 0)), tr(VT_ROWS)],
        out_shape=[tr_shape(LANES), jax.ShapeDtypeStruct((n, ATT_DIM), BF16), tr_shape(VT_ROWS)],
        compiler_params=_params("parallel"),
        name="qkv_rope",
    )(x, nw.reshape(1, d), w_bf16, *tables)


def _attn_body(qt_ref, k_ref, vt_ref, lam_ref, sw_ref, o_ref, m_scr, acc_scr, *, lambda_init):
    tq = ATT_TILE
    qi = pl.program_id(2)
    feat = lax.broadcasted_iota(jnp.int32, (LANES, tq), 0)
    rk = lax.broadcasted_iota(jnp.int32, (tq, 2 * tq), 0)
    rq = lax.broadcasted_iota(jnp.int32, (tq, 2 * tq), 1)
    rq = jnp.where(rq >= tq, rq - tq, rq)
    visible = (rk // CHUNK) <= (rq // CHUNK)

    qs = []
    for g in range(ATT_GROUP):
        qt = qt_ref[0, g, 0]
        zero = jnp.zeros_like(qt)
        qs.append(jnp.concatenate([jnp.where(feat < ATT_HEAD_DIM, qt, zero),
                                   jnp.where(feat < ATT_HEAD_DIM, zero, qt)], axis=1))

    def scores(g, j, ntile):
        off = pl.multiple_of(j * tq, tq)
        return jnp.dot(k_ref[0, pl.ds(off, ntile * tq), g * LANES:(g + 1) * LANES], qs[g],
                       preferred_element_type=F32)

    def pv(g, j, ntile, p):
        out = jnp.dot(vt_ref[0, g, j], p[0:tq], preferred_element_type=F32)
        for i in range(1, ntile):
            out = out + jnp.dot(vt_ref[0, g, j + i], p[i * tq:(i + 1) * tq], preferred_element_type=F32)
        return out

    for g in range(ATT_GROUP):
        s = jnp.where(visible, scores(g, qi, 1), -jnp.inf)
        m0 = jnp.max(s, axis=0, keepdims=True)
        m_scr[g] = m0
        acc_scr[g] = pv(g, qi, 1, jnp.exp2(s - m0).astype(BF16))

    def update(j, ntile):
        for g in range(ATT_GROUP):
            s = scores(g, j, ntile)
            m_old = m_scr[g]
            m_new = jnp.maximum(m_old, jnp.max(s, axis=0, keepdims=True))
            m_scr[g] = m_new
            acc_scr[g] = jnp.exp2(m_old - m_new) * acc_scr[g] + pv(g, j, ntile, jnp.exp2(s - m_new).astype(BF16))

    @pl.when(qi % 2 == 1)
    def _():
        update(0, 1)

    def body(t, carry):
        update(qi % 2 + 2 * t, 2)
        return carry

    lax.fori_loop(0, qi // 2, body, 0)

    lp = lam_ref[...]
    lam = (jnp.exp(jnp.sum(lp[0:1] * lp[1:2], axis=-1, keepdims=True))
           - jnp.exp(jnp.sum(lp[2:3] * lp[3:4], axis=-1, keepdims=True)) + lambda_init)
    for g in range(ATT_GROUP):
        acc = acc_scr[g]
        on = acc[0:LANES, :] / acc[LANES:LANES + 1, :]
        o = (on[:, 0:tq] - lam * on[:, tq:2 * tq]).T
        o_ref[0, :, g * LANES:(g + 1) * LANES] = (_rms(o, sw_ref[...]) * (1.0 - lambda_init)).astype(o_ref.dtype)


def diff_attention(qt, k, vt, lam_p, subln_w, lambda_init):
    b, t, _ = k.shape
    tq = ATT_TILE
    tiles = t // tq
    g = ATT_GROUP
    return pl.pallas_call(
        functools.partial(_attn_body, lambda_init=lambda_init),
        grid=(b, ATT_HEADS // g, tiles),
        in_specs=[pl.BlockSpec((1, g, 1, LANES, tq), lambda i, h, j: (i, h, j, 0, 0)),
                  pl.BlockSpec((1, t, g * LANES), lambda i, h, j: (i, 0, h)),
                  pl.BlockSpec((1, g, tiles, VT_ROWS, tq), lambda i, h, j: (i, h, 0, 0, 0)),
                  pl.BlockSpec((4, ATT_HEAD_DIM), lambda i, h, j: (0, 0)),
                  pl.BlockSpec((1, LANES), lambda i, h, j: (0, 0))],
        out_specs=pl.BlockSpec((1, tq, g * LANES), lambda i, h, j: (i, j, h)),
        out_shape=jax.ShapeDtypeStruct((b, t, ATT_DIM), BF16),
        scratch_shapes=[pltpu.VMEM((g, 1, 2 * tq), F32), pltpu.VMEM((g, VT_ROWS, 2 * tq), F32)],
        compiler_params=_params("parallel", "parallel", "arbitrary"),
        name="diff_attention",
    )(qt, k, vt, lam_p, subln_w.reshape(1, LANES))


def _matmul_res_body(a_ref, w_ref, h_ref, o_ref):
    o_ref[...] = h_ref[...] + jnp.dot(a_ref[...], w_ref[...], preferred_element_type=F32)


def matmul_residual(a_bf16, w_bf16, h):
    n, kdim = a_bf16.shape
    d = w_bf16.shape[1]
    return pl.pallas_call(
        _matmul_res_body,
        grid=(n // ROW_TILE,),
        in_specs=[pl.BlockSpec((ROW_TILE, kdim), lambda i: (i, 0)),
                  pl.BlockSpec((kdim, d), lambda i: (0, 0)),
                  pl.BlockSpec((ROW_TILE, d), lambda i: (i, 0))],
        out_specs=pl.BlockSpec((ROW_TILE, d), lambda i: (i, 0)),
        out_shape=jax.ShapeDtypeStruct((n, d), F32),
        compiler_params=_params("parallel"),
        name="matmul_residual",
    )(a_bf16, w_bf16, h)


def odd_layer(h, nw, w_qkv, lam_p, subln_w, w_out, lambda_init, tables):
    b, t, d = h.shape
    qt, k, vt = qkv_rope(h.reshape(b * t, d), nw, w_qkv.astype(BF16), tables, b, t)
    o = diff_attention(qt, k.reshape(b, t, ATT_DIM), vt, lam_p.astype(F32), subln_w.astype(F32), lambda_init)
    return matmul_residual(o.reshape(b * t, ATT_DIM), w_out.astype(BF16), h.reshape(b * t, d)).reshape(b, t, d)


META_E, META_R, META_G = 0, 2, 4


def _router_body(h_ref, nw_ref, wr_ref, br_ref, meta_ref, cnt_ref, run_cnt):
    tm = MOE_TILE

    @pl.when(pl.program_id(0) == 0)
    def _():
        run_cnt[...] = jnp.zeros_like(run_cnt)

    xn = _rms(h_ref[...], nw_ref[...])
    logits = jnp.dot(xn, wr_ref[...], preferred_element_type=F32, precision=lax.Precision.HIGHEST) + br_ref[...]
    lane = lax.broadcasted_iota(jnp.int32, (tm, LANES), 1)
    big = jnp.int32(LANES)
    neg = -jnp.inf

    is_g = lane < MOE_GROUPS
    gl = jnp.where(is_g, logits, neg)
    gexp = jnp.exp(gl - jnp.max(gl, axis=-1, keepdims=True))
    gprob = gexp / jnp.sum(gexp, axis=-1, keepdims=True)
    g_gate = jnp.max(gprob, axis=-1, keepdims=True)
    g_idx = jnp.min(jnp.where(is_g & (gprob == g_gate), lane, big), axis=-1, keepdims=True)

    lo = ROUTER_LANE0 + g_idx * EXPERTS_PER_GROUP
    sel = (lane >= lo) & (lane < lo + EXPERTS_PER_GROUP)
    el = jnp.where(sel, logits, neg)
    eexp = jnp.exp(el - jnp.max(el, axis=-1, keepdims=True))
    eprob = jnp.where(sel, eexp / jnp.sum(eexp, axis=-1, keepdims=True), -1.0)
    p1 = jnp.max(eprob, axis=-1, keepdims=True)
    i1 = jnp.min(jnp.where(eprob == p1, lane, big), axis=-1, keepdims=True)
    rest = jnp.where(lane == i1, -1.0, eprob)
    p2 = jnp.max(rest, axis=-1, keepdims=True)
    i2 = jnp.min(jnp.where(rest == p2, lane, big), axis=-1, keepdims=True)
    g1 = g_gate * p1 / (p1 + p2)
    g2 = g_gate * p2 / (p1 + p2)

    hit1 = lane == i1
    hit2 = lane == i2
    assign = jnp.where(hit1 | hit2, 1.0, 0.0)
    rr = lax.broadcasted_iota(jnp.int32, (tm, tm), 0)
    cc = lax.broadcasted_iota(jnp.int32, (tm, tm), 1)
    before = jnp.where(cc < rr, 1.0, 0.0).astype(BF16)
    prior = jnp.dot(before, assign.astype(BF16), preferred_element_type=F32) + run_cnt[...]
    r1 = jnp.sum(jnp.where(hit1, prior, 0.0), axis=-1, keepdims=True)
    r2 = jnp.sum(jnp.where(hit2, prior, 0.0), axis=-1, keepdims=True)
    run_cnt[...] = run_cnt[...] + jnp.sum(assign, axis=0, keepdims=True)
    cnt_ref[...] = jnp.broadcast_to(run_cnt[...], cnt_ref.shape)

    e1 = (i1 - ROUTER_LANE0).astype(F32)
    e2 = (i2 - ROUTER_LANE0).astype(F32)
    rec = jnp.zeros((tm, LANES), F32)
    for pos, val in ((META_E, e1), (META_E + 1, e2), (META_R, r1), (META_R + 1, r2), (META_G, g1), (META_G + 1, g2)):
        rec = jnp.where(lane == pos, val, rec)
    meta_ref[...] = rec


def moe_router(h, nw, w_router, b_router):
    n, d = h.shape
    tm = MOE_TILE
    return pl.pallas_call(
        _router_body,
        grid=(n // tm,),
        in_specs=[pl.BlockSpec((tm, d), lambda i: (i, 0)),
                  pl.BlockSpec((1, d), lambda i: (0, 0)),
                  pl.BlockSpec((d, LANES), lambda i: (0, 0)),
                  pl.BlockSpec((1, LANES), lambda i: (0, 0))],
        out_specs=[pl.BlockSpec((tm, LANES), lambda i: (i, 0)),
                   pl.BlockSpec((SUBLANES, LANES), lambda i: (0, 0))],
        out_shape=[jax.ShapeDtypeStruct((n, LANES), F32), jax.ShapeDtypeStruct((SUBLANES, LANES), F32)],
        scratch_shapes=[pltpu.VMEM((1, LANES), F32)],
        compiler_params=_params("arbitrary"),
        name="moe_router",
    )(h, nw.reshape(1, d), w_router, b_router)


def _row_copy(src, src_row, dst, dst_row, sem):
    return pltpu.make_async_copy(src.at[pl.ds(src_row, 1), :], dst.at[pl.ds(dst_row, 1), :], sem)


def _dispatch_body(dest_ref, h_ref, nw_ref, zeros_ref, xrows_ref, xn_scr, sem):
    del zeros_ref
    tm = MOE_TILE
    xn_scr[...] = _rms(h_ref[...], nw_ref[...])

    def issue(i, carry):
        _row_copy(xn_scr, i, xrows_ref, dest_ref[0, 0, 2 * i], sem).start()
        _row_copy(xn_scr, i, xrows_ref, dest_ref[0, 0, 2 * i + 1], sem).start()
        return carry

    lax.fori_loop(0, tm, issue, 0, unroll=DMA_UNROLL)
    for _ in range(2):
        pltpu.make_async_copy(xn_scr, xrows_ref.at[pl.ds(0, tm), :], sem).wait()


def moe_dispatch(h, nw, dest, n_rows):
    n, d = h.shape
    tm = MOE_TILE
    zeros = jnp.zeros((n_rows, d), F32)
    return pl.pallas_call(
        _dispatch_body,
        grid=(n // tm,),
        in_specs=[pl.BlockSpec((1, 1, 2 * tm), lambda i: (i, 0, 0), memory_space=pltpu.SMEM),
                  pl.BlockSpec((tm, d), lambda i: (i, 0)),
                  pl.BlockSpec((1, d), lambda i: (0, 0)),
                  pl.BlockSpec(memory_space=pl.ANY)],
        out_specs=pl.BlockSpec(memory_space=pl.ANY),
        out_shape=jax.ShapeDtypeStruct((n_rows, d), F32),
        scratch_shapes=[pltpu.VMEM((tm, d), F32), pltpu.SemaphoreType.DMA],
        input_output_aliases={3: 0},
        compiler_params=_params("arbitrary"),
        name="moe_dispatch",
    )(dest.reshape(n // tm, 1, 2 * tm), h, nw.reshape(1, d), zeros)


def _ffn_body(blk_e_ref, n_used_ref, x_ref, w1_ref, w3_ref, w2_ref, y_ref, w1b, w3b, w2b):
    b = pl.program_id(0)
    prev = blk_e_ref[jnp.maximum(b - 1, 0)]
    active = b < n_used_ref[0]

    @pl.when(active & ((b == 0) | (blk_e_ref[b] != prev)))
    def _():
        w1b[...] = w1_ref[0, 0].astype(BF16)
        w3b[...] = w3_ref[0, 0].astype(BF16)
        w2b[...] = w2_ref[0, 0].astype(BF16)

    @pl.when(active)
    def _():
        x = x_ref[...].astype(BF16)
        u = jnp.dot(x, w1b[...], preferred_element_type=F32)
        g = jnp.dot(x, w3b[...], preferred_element_type=F32)
        y_ref[...] = jnp.dot((_silu(u) * g).astype(BF16), w2b[...], preferred_element_type=F32)

    @pl.when(jnp.logical_not(active))
    def _():
        y_ref[...] = jnp.zeros_like(y_ref)


def moe_ffn(x_rows, blk_expert, n_used, w1, w3, w2, layer):
    n_rows, d = x_rows.shape
    de = w1.shape[-1]
    n_blk = n_rows // MOE_BLOCK
    row = lambda b, be, nu: (b, 0)
    wsel = lambda b, be, nu: (layer, be[b], 0, 0)
    return pl.pallas_call(
        _ffn_body,
        grid_spec=pltpu.PrefetchScalarGridSpec(
            num_scalar_prefetch=2,
            grid=(n_blk,),
            in_specs=[pl.BlockSpec((MOE_BLOCK, d), row),
                      pl.BlockSpec((1, 1, d, de), wsel),
                      pl.BlockSpec((1, 1, d, de), wsel),
                      pl.BlockSpec((1, 1, de, d), wsel)],
            out_specs=pl.BlockSpec((MOE_BLOCK, d), row),
            scratch_shapes=[pltpu.VMEM((d, de), BF16), pltpu.VMEM((d, de), BF16), pltpu.VMEM((de, d), BF16)]),
        out_shape=jax.ShapeDtypeStruct((n_rows, d), F32),
        compiler_params=_params("arbitrary"),
        name="moe_ffn",
    )(blk_expert, n_used, x_rows, w1, w3, w2)


def _combine_body(dest_ref, meta_ref, h_ref, yrows_ref, fw_ref, o_ref, ybuf, sem, *, final_norm):
    tm = MOE_TILE

    def issue(i, carry):
        _row_copy(yrows_ref, dest_ref[0, 0, 2 * i], ybuf.at[0], i, sem).start()
        _row_copy(yrows_ref, dest_ref[0, 0, 2 * i + 1], ybuf.at[1], i, sem).start()
        return carry

    lax.fori_loop(0, tm, issue, 0, unroll=DMA_UNROLL)
    for half in range(2):
        pltpu.make_async_copy(yrows_ref.at[pl.ds(0, tm), :], ybuf.at[half], sem).wait()

    meta = meta_ref[...]
    out = h_ref[...] + (meta[:, META_G:META_G + 1] * ybuf[0] + meta[:, META_G + 1:META_G + 2] * ybuf[1])
    if final_norm:
        out = _rms(out, fw_ref[...])
    o_ref[...] = out


def moe_combine(h, meta, dest, y_rows, final_w):
    n, d = h.shape
    tm = MOE_TILE
    final_norm = final_w is not None
    fw = (final_w if final_norm else jnp.ones((d,), F32)).reshape(1, d)
    return pl.pallas_call(
        functools.partial(_combine_body, final_norm=final_norm),
        grid=(n // tm,),
        in_specs=[pl.BlockSpec((1, 1, 2 * tm), lambda i: (i, 0, 0), memory_space=pltpu.SMEM),
                  pl.BlockSpec((tm, LANES), lambda i: (i, 0)),
                  pl.BlockSpec((tm, d), lambda i: (i, 0)),
                  pl.BlockSpec(memory_space=pl.ANY),
                  pl.BlockSpec((1, d), lambda i: (0, 0))],
        out_specs=pl.BlockSpec((tm, d), lambda i: (i, 0)),
        out_shape=jax.ShapeDtypeStruct((n, d), F32),
        scratch_shapes=[pltpu.VMEM((2, tm, d), F32), pltpu.SemaphoreType.DMA],
        compiler_params=_params("arbitrary"),
        name="moe_combine",
    )(dest.reshape(n // tm, 1, 2 * tm), meta, h, y_rows, fw)


def hier_moe_layer(h, nw, w_group, b_group, w_expert, b_expert, w1, w3, w2, layer, final_w):
    n, d = h.shape
    pad_cols = LANES - MOE_GROUPS - N_EXPERTS
    w_router = jnp.concatenate([w_group, w_expert, jnp.zeros((d, pad_cols), F32)], axis=1)
    b_router = jnp.concatenate([b_group, b_expert, jnp.zeros((pad_cols,), F32)]).reshape(1, LANES)
    meta, cnt = moe_router(h, nw, w_router, b_router)

    counts = cnt[0, ROUTER_LANE0:ROUTER_LANE0 + N_EXPERTS].astype(jnp.int32)
    padded = (counts + MOE_BLOCK - 1) // MOE_BLOCK * MOE_BLOCK
    pends = jnp.cumsum(padded)
    pstarts = pends - padded
    expert = meta[:, META_E:META_E + 2].astype(jnp.int32)
    rank = meta[:, META_R:META_R + 2].astype(jnp.int32)
    dest = (pstarts[expert] + rank).reshape(n * 2)
    n_rows = n * 2 + N_EXPERTS * MOE_BLOCK
    n_blk = n_rows // MOE_BLOCK
    blk_start = jnp.arange(n_blk, dtype=jnp.int32) * MOE_BLOCK
    blk_expert = jnp.minimum(jnp.sum((pends[None, :] <= blk_start[:, None]).astype(jnp.int32), axis=1),
                             N_EXPERTS - 1)
    n_used = (pends[-1:] // MOE_BLOCK).astype(jnp.int32)

    x_rows = moe_dispatch(h, nw, dest, n_rows)
    y_rows = moe_ffn(x_rows, blk_expert, n_used, w1, w3, w2, layer)
    return moe_combine(h, meta, dest, y_rows, final_w)


def _lambda_init(layer):
    return 0.8 - 0.6 * math.exp(-0.3 * layer)


def kernel(x, norm_mix_w, norm_ffn_w, final_norm_w, ev_w_in, ev_conv_w, ev_conv_b, ev_dt_bias, ev_a_log,
           ev_d_skip, ev_gnorm_w, ev_sconv_w, ev_w_out, od_w_qkv, od_lambda, od_subln_w, od_w_out,
           moe_w_group, moe_b_group, moe_w_expert, moe_b_expert, moe_w1, moe_w3, moe_w2):
    b, t, d = x.shape
    depth = norm_mix_w.shape[0]
    tables = _rope_tables(t)
    h = x
    for layer in range(depth):
        i = layer // 2
        if layer % 2 == 0:
            h = even_layer(h, norm_mix_w[layer], ev_w_in[i], ev_conv_w[i], ev_conv_b[i], ev_dt_bias[i],
                           ev_a_log[i], ev_d_skip[i], ev_gnorm_w[i], ev_sconv_w[i], ev_w_out[i])
        else:
            h = odd_layer(h, norm_mix_w[layer], od_w_qkv[i], od_lambda[i], od_subln_w[i], od_w_out[i],
                          _lambda_init(layer), tables)
        final_w = final_norm_w if layer == depth - 1 else None
        h = hier_moe_layer(h.reshape(b * t, d), norm_ffn_w[layer], moe_w_group[layer], moe_b_group[layer],
                           moe_w_expert[layer], moe_b_expert[layer], moe_w1, moe_w3, moe_w2, layer,
                           final_w).reshape(b, t, d)
    return h
```

```python
import functools
import math

import jax
import jax.numpy as jnp
from jax import lax
from jax.experimental import pallas as pl
from jax.experimental.pallas import tpu as pltpu

F32 = jnp.float32
BF16 = jnp.bfloat16

D_MODEL = 1024
CHUNK = 64
SSM_INNER = 1536
SSM_HEAD_DIM = 64
SSM_HEADS = 24
SSM_GROUPS = 4
SSM_STATE = 128
SSM_CONV = 4
SSM_CONV_CH = SSM_INNER + 2 * SSM_GROUPS * SSM_STATE
SSM_PAIRS = SSM_HEADS // 2
PAIRS_PER_GROUP = SSM_PAIRS // SSM_GROUPS
GROUP_CH = SSM_INNER // SSM_GROUPS
SCONV_DIM = 512
SCONV_WIDTH = 3
EVEN_MIX = SSM_INNER + SCONV_DIM
ATT_HEADS = 8
ATT_HEAD_DIM = 64
ATT_DIM = ATT_HEADS * 2 * ATT_HEAD_DIM
ROPE_DIM = ATT_HEAD_DIM // 4
ROPE_THETA = 500000.0
MOE_GROUPS = 8
EXPERTS_PER_GROUP = 8
N_EXPERTS = 64
D_EXPERT = 512
EPS = 1e-6

LANES = 128
SUBLANES = 8
VMEM_LIMIT = 56 * 1024 * 1024

OFF_Z = 0
OFF_XBC = SSM_INNER
OFF_HB = OFF_XBC + SSM_CONV_CH
OFF_BG = OFF_HB + SCONV_DIM
OFF_CG = OFF_BG + SCONV_DIM
OFF_DT = OFF_CG + SCONV_DIM
EVEN_PROJ = OFF_DT + LANES

ROW_TILE = 256
SSD_CHUNK = 256
ATT_TILE = 256
ATT_GROUP = 4
VT_ROWS = LANES + 16
LOG2E = 1.4426950408889634
MOE_TILE = 512
MOE_BLOCK = 256
PACKED = D_MODEL // 2
DMA_UNROLL = 8
ROUTER_LANE0 = MOE_GROUPS


def _params(*sem):
    return pltpu.CompilerParams(dimension_semantics=sem, vmem_limit_bytes=VMEM_LIMIT)


def _rms(x, w):
    return x * lax.rsqrt(jnp.mean(x * x, axis=-1, keepdims=True) + EPS) * w


def _silu(x):
    return x * (1.0 / (1.0 + jnp.exp(-x)))


def _norm_matmul_body(x_ref, nw_ref, w_ref, o_ref, *, col_chunk):
    xn = _rms(x_ref[...], nw_ref[...]).astype(BF16)
    for c0 in range(0, o_ref.shape[-1], col_chunk):
        o_ref[:, c0:c0 + col_chunk] = jnp.dot(
            xn, w_ref[:, c0:c0 + col_chunk], preferred_element_type=F32).astype(o_ref.dtype)


def norm_matmul(x, nw, w_bf16, col_chunk):
    n, d = x.shape
    f = w_bf16.shape[1]
    return pl.pallas_call(
        functools.partial(_norm_matmul_body, col_chunk=col_chunk),
        grid=(n // ROW_TILE,),
        in_specs=[pl.BlockSpec((ROW_TILE, d), lambda i: (i, 0)),
                  pl.BlockSpec((1, d), lambda i: (0, 0)),
                  pl.BlockSpec((d, f), lambda i: (0, 0))],
        out_specs=pl.BlockSpec((ROW_TILE, f), lambda i: (i, 0)),
        out_shape=jax.ShapeDtypeStruct((n, f), F32),
        compiler_params=_params("parallel"),
        name="norm_matmul",
    )(x, nw.reshape(1, d), w_bf16)


def _softplus(x):
    return jnp.maximum(x, 0.0) + jnp.log(1.0 + jnp.exp(-jnp.abs(x)))


def _ssd_body(proj_ref, h_ref, cw_ref, cb_ref, dtb_ref, aneg_ref, dsk_ref, gnw_ref, scw_ref, wout_ref,
              o_ref, cbuf, xact, mix, state, sbuf):
    tc = ROW_TILE
    halo = SUBLANES

    @pl.when(pl.program_id(1) == 0)
    def _():
        state[...] = jnp.zeros_like(state)
        cbuf[0:halo, :] = jnp.zeros((halo, SSM_CONV_CH), F32)
        sbuf[0:halo, :] = jnp.zeros((halo, SCONV_DIM), F32)

    cbuf[halo:halo + tc, :] = proj_ref[0, :, OFF_XBC:OFF_XBC + SSM_CONV_CH]
    for c0 in range(0, SSM_CONV_CH, 512):
        xin = cbuf[:, c0:c0 + 512]
        acc = cb_ref[:, c0:c0 + 512] + cw_ref[SSM_CONV - 1:SSM_CONV, c0:c0 + 512] * xin
        for back in range(1, SSM_CONV):
            j = SSM_CONV - 1 - back
            acc = acc + cw_ref[j:j + 1, c0:c0 + 512] * pltpu.roll(xin, back, 0)
        xact[:, c0:c0 + 512] = _silu(acc[halo:halo + tc])
    cbuf[0:halo, :] = cbuf[tc:tc + halo, :]

    sbuf[halo:halo + tc, :] = (proj_ref[0, :, OFF_CG:OFF_CG + SCONV_DIM]
                               * proj_ref[0, :, OFF_HB:OFF_HB + SCONV_DIM])
    sin = sbuf[...]
    conv = scw_ref[SCONV_WIDTH - 1:SCONV_WIDTH, :] * sin
    for back in range(1, SCONV_WIDTH):
        j = SCONV_WIDTH - 1 - back
        conv = conv + scw_ref[j:j + 1, :] * pltpu.roll(sin, back, 0)
    mix[:, SSM_INNER:EVEN_MIX] = (proj_ref[0, :, OFF_BG:OFF_BG + SCONV_DIM] * conv[halo:halo + tc]).astype(BF16)
    sbuf[0:halo, :] = sbuf[tc:tc + halo, :]

    L = SSD_CHUNK
    lane_p = lax.broadcasted_iota(jnp.int32, (L, LANES), 1)
    p_lo = lane_p < SSM_HEAD_DIM
    sub_s = lax.broadcasted_iota(jnp.int32, (L, 2 * L), 0)
    lane_s = lax.broadcasted_iota(jnp.int32, (L, 2 * L), 1)
    s_lo = lane_s < L
    tril2 = sub_s >= jnp.where(s_lo, lane_s, lane_s - L)
    csum_l = lax.broadcasted_iota(jnp.int32, (L, L), 0)
    csum_s = lax.broadcasted_iota(jnp.int32, (L, L), 1)
    tril_incl = (csum_s <= csum_l).astype(F32)
    sub2 = lax.broadcasted_iota(jnp.int32, (LANES, LANES), 0)
    rhs_top = lax.broadcasted_iota(jnp.int32, (2 * L, LANES), 0) < L
    rhs_lo = lax.broadcasted_iota(jnp.int32, (2 * L, LANES), 1) < SSM_HEAD_DIM

    def chunk_body(c, carry):
        r0 = pl.multiple_of(c * L, L)
        rows = pl.ds(r0, L)
        dt = _softplus(proj_ref[0, rows, OFF_DT:OFF_DT + LANES] + dtb_ref[...])
        a = dt * aneg_ref[...]
        acum = jnp.dot(tril_incl, a, preferred_element_type=F32, precision=lax.Precision.HIGHEST)
        acum_t = acum.T
        for g in range(SSM_GROUPS):
            bg = xact[rows, SSM_INNER + g * SSM_STATE:SSM_INNER + (g + 1) * SSM_STATE].astype(BF16)
            c_off = SSM_INNER + SSM_GROUPS * SSM_STATE
            cg = xact[rows, c_off + g * SSM_STATE:c_off + (g + 1) * SSM_STATE].astype(BF16)
            cb = lax.dot_general(cg, bg, (((1,), (1,)), ((), ())), preferred_element_type=F32)
            cb2 = jnp.concatenate([cb, cb], axis=1)
            ys = []
            for jj in range(PAIRS_PER_GROUP):
                j = g * PAIRS_PER_GROUP + jj
                h0, h1 = 2 * j, 2 * j + 1
                xp = xact[rows, j * LANES:(j + 1) * LANES]
                dcol0, dcol1 = dt[:, h0:h0 + 1], dt[:, h1:h1 + 1]
                acol0, acol1 = acum[:, h0:h0 + 1], acum[:, h1:h1 + 1]
                dt_p = jnp.where(p_lo, dcol0, dcol1)
                col_p = jnp.where(p_lo, acol0, acol1)
                col_s = jnp.where(s_lo, acol0, acol1)
                row_s = jnp.concatenate([acum_t[h0:h0 + 1, :], acum_t[h1:h1 + 1, :]], axis=1)
                last_p = col_p[L - 1:L, :]
                xdt = xp * dt_p
                m2 = cb2 * jnp.exp(jnp.where(tril2, col_s - row_s, -jnp.inf))
                xdt2 = jnp.concatenate([xdt, xdt], axis=0)
                rhs = jnp.where(rhs_top == rhs_lo, xdt2, 0.0).astype(BF16)
                y2 = jnp.dot(m2.astype(BF16), rhs, preferred_element_type=F32)
                s_old = state[j]
                y_in = lax.dot_general(cg, s_old.astype(BF16), (((1,), (1,)), ((), ())),
                                       preferred_element_type=F32)
                y2 = y2 + y_in * jnp.exp(col_p) + dsk_ref[:, j * LANES:(j + 1) * LANES] * xp
                xd = (xdt * jnp.exp(last_p - col_p)).astype(BF16)
                upd = lax.dot_general(xd, bg, (((0,), (0,)), ((), ())), preferred_element_type=F32)
                e0 = jnp.exp(acum_t[h0:h0 + 1, L - 1:L])
                e1 = jnp.exp(acum_t[h1:h1 + 1, L - 1:L])
                state[j] = s_old * jnp.where(sub2 < SSM_HEAD_DIM, e0, e1) + upd
                ys.append(y2)
            yg = jnp.concatenate(ys, axis=1)
            yg = yg * _silu(proj_ref[0, rows, OFF_Z + g * GROUP_CH:OFF_Z + (g + 1) * GROUP_CH])
            mix[rows, g * GROUP_CH:(g + 1) * GROUP_CH] = _rms(
                yg, gnw_ref[:, g * GROUP_CH:(g + 1) * GROUP_CH]).astype(BF16)
        return carry

    lax.fori_loop(0, tc // SSD_CHUNK, chunk_body, 0)
    o_ref[0] = h_ref[0] + jnp.dot(mix[...], wout_ref[...], preferred_element_type=F32)


def ssd_mixer(proj, h, cw, cb, dtb, aneg, dsk, gnw, scw, wout_bf16):
    b, t, _ = h.shape
    tc = ROW_TILE
    const = lambda shape: pl.BlockSpec(shape, lambda i, j: (0,) * len(shape))
    return pl.pallas_call(
        _ssd_body,
        grid=(b, t // tc),
        in_specs=[pl.BlockSpec((1, tc, EVEN_PROJ), lambda i, j: (i, j, 0)),
                  pl.BlockSpec((1, tc, D_MODEL), lambda i, j: (i, j, 0)),
                  const((SSM_CONV, SSM_CONV_CH)), const((1, SSM_CONV_CH)),
                  const((1, LANES)), const((1, LANES)), const((1, SSM_INNER)), const((1, SSM_INNER)),
                  const((SCONV_WIDTH, SCONV_DIM)), const((EVEN_MIX, D_MODEL))],
        out_specs=pl.BlockSpec((1, tc, D_MODEL), lambda i, j: (i, j, 0)),
        out_shape=jax.ShapeDtypeStruct((b, t, D_MODEL), F32),
        scratch_shapes=[pltpu.VMEM((tc + SUBLANES, SSM_CONV_CH), F32),
                        pltpu.VMEM((tc, SSM_CONV_CH), F32),
                        pltpu.VMEM((tc, EVEN_MIX), BF16),
                        pltpu.VMEM((SSM_PAIRS, LANES, SSM_STATE), F32),
                        pltpu.VMEM((tc + SUBLANES, SCONV_DIM), F32)],
        compiler_params=_params("parallel", "arbitrary"),
        name="ssd_mixer",
    )(proj, h, cw, cb, dtb, aneg, dsk, gnw, scw, wout_bf16)


def even_layer(h, nw, w_in, conv_w, conv_b, dt_bias, a_log, d_skip, gnorm_w, sconv_w, w_out):
    b, t, d = h.shape
    o1 = SSM_INNER
    o2 = o1 + SSM_CONV_CH
    o3 = o2 + SSM_HEADS
    o4 = o3 + SCONV_DIM
    o5 = o4 + SCONV_DIM
    pad = jnp.zeros((d, LANES - SSM_HEADS), F32)
    w_re = jnp.concatenate([w_in[:, :o2], w_in[:, o3:], w_in[:, o2:o3], pad], axis=1).astype(BF16)
    proj = norm_matmul(h.reshape(b * t, d), nw, w_re, 640).reshape(b, t, EVEN_PROJ)
    lane_pad = lambda v: jnp.concatenate([v.astype(F32), jnp.zeros((LANES - SSM_HEADS,), F32)]).reshape(1, LANES)
    aneg = lane_pad(-jnp.exp(a_log.astype(F32)))
    dsk = jnp.repeat(d_skip.astype(F32), SSM_HEAD_DIM).reshape(1, SSM_INNER)
    del o4, o5
    return ssd_mixer(proj, h, conv_w, conv_b.reshape(1, -1), lane_pad(dt_bias), aneg, dsk,
                     gnorm_w.reshape(1, -1), sconv_w, w_out.astype(BF16))


def _qkv_body(x_ref, nw_ref, w_ref, cos_ref, sa_ref, sb_ref, qt_ref, k_ref, vt_ref):
    xn = _rms(x_ref[...], nw_ref[...]).astype(BF16)
    cos, sa, sb = cos_ref[...], sa_ref[...], sb_ref[...]

    def rope(a):
        return a * cos + pltpu.roll(a, ROPE_DIM // 2, 1) * sa + pltpu.roll(a, LANES - ROPE_DIM // 2, 1) * sb

    for c2 in range(0, ATT_HEADS, 2):
        proj = lambda part: jnp.dot(xn, w_ref[:, part * ATT_DIM + c2 * LANES:part * ATT_DIM + (c2 + 2) * LANES],
                                    preferred_element_type=F32)
        q2, k2, v2 = proj(0), proj(1), proj(2)
        for i in range(2):
            c = c2 + i
            lanes = slice(i * LANES, (i + 1) * LANES)
            qt_ref[0, c, 0] = (rope(q2[:, lanes]) * (ATT_HEAD_DIM ** -0.5 * LOG2E)).astype(BF16).T
            k_ref[:, c * LANES:(c + 1) * LANES] = rope(k2[:, lanes]).astype(BF16)
            vt_ref[0, c, 0, 0:LANES, :] = v2[:, lanes].astype(BF16).T
            row = lax.broadcasted_iota(jnp.int32, (VT_ROWS - LANES, ROW_TILE), 0)
            vt_ref[0, c, 0, LANES:VT_ROWS, :] = jnp.where(row == 0, 1.0, 0.0).astype(BF16)


def _rope_tables(t):
    half = ROPE_DIM // 2
    inv = 1.0 / (ROPE_THETA ** (jnp.arange(0, ROPE_DIM, 2, dtype=F32) / ROPE_DIM))
    ang = jnp.arange(t, dtype=F32)[:, None] * inv[None, :]
    cos, sin = jnp.cos(ang), jnp.sin(ang)
    rest = ATT_HEAD_DIM - ROPE_DIM
    one = jnp.ones((t, rest), F32)
    zero = jnp.zeros((t, rest), F32)
    zh = jnp.zeros((t, half), F32)
    cos64 = jnp.concatenate([cos, cos, one], axis=1)
    sa64 = jnp.concatenate([zh, sin, zero], axis=1)
    sb64 = jnp.concatenate([-sin, zh, zero], axis=1)
    tile2 = lambda v: jnp.concatenate([v, v], axis=1)
    return tile2(cos64), tile2(sa64), tile2(sb64)


def qkv_rope(x, nw, w_bf16, tables, b, t):
    n, d = x.shape
    assert ROW_TILE == ATT_TILE
    tiles = t // ROW_TILE
    tab = pl.BlockSpec((ROW_TILE, LANES), lambda i: (i % tiles, 0))
    tr = lambda rows: pl.BlockSpec((1, ATT_HEADS, 1, rows, ROW_TILE), lambda i: (i // tiles, 0, i % tiles, 0, 0))
    tr_shape = lambda rows: jax.ShapeDtypeStruct((b, ATT_HEADS, tiles, rows, ROW_TILE), BF16)
    return pl.pallas_call(
        _qkv_body,
        grid=(n // ROW_TILE,),
        in_specs=[pl.BlockSpec((ROW_TILE, d), lambda i: (i, 0)),
                  pl.BlockSpec((1, d), lambda i: (0, 0)),
                  pl.BlockSpec((d, 3 * ATT_DIM), lambda i: (0, 0)),
                  tab, tab, tab],
        out_specs=[tr(LANES), pl.BlockSpec((ROW_TILE, ATT_DIM), lambda i: (i, 0)), tr(VT_ROWS)],
        out_shape=[tr_shape(LANES), jax.ShapeDtypeStruct((n, ATT_DIM), BF16), tr_shape(VT_ROWS)],
        compiler_params=_params("parallel"),
        name="qkv_rope",
    )(x, nw.reshape(1, d), w_bf16, *tables)


def _attn_body(qt_ref, k_ref, vt_ref, lam_ref, sw_ref, o_ref, m_scr, acc_scr, *, lambda_init):
    tq = ATT_TILE
    qi = pl.program_id(2)
    feat = lax.broadcasted_iota(jnp.int32, (LANES, tq), 0)
    rk = lax.broadcasted_iota(jnp.int32, (tq, 2 * tq), 0)
    rq = lax.broadcasted_iota(jnp.int32, (tq, 2 * tq), 1)
    rq = jnp.where(rq >= tq, rq - tq, rq)
    visible = (rk // CHUNK) <= (rq // CHUNK)

    qs = []
    for g in range(ATT_GROUP):
        qt = qt_ref[0, g, 0]
        zero = jnp.zeros_like(qt)
        qs.append(jnp.concatenate([jnp.where(feat < ATT_HEAD_DIM, qt, zero),
                                   jnp.where(feat < ATT_HEAD_DIM, zero, qt)], axis=1))

    def scores(g, j, ntile):
        off = pl.multiple_of(j * tq, tq)
        return jnp.dot(k_ref[0, pl.ds(off, ntile * tq), g * LANES:(g + 1) * LANES], qs[g],
                       preferred_element_type=F32)

    def pv(g, j, ntile, p):
        out = jnp.dot(vt_ref[0, g, j], p[0:tq], preferred_element_type=F32)
        for i in range(1, ntile):
            out = out + jnp.dot(vt_ref[0, g, j + i], p[i * tq:(i + 1) * tq], preferred_element_type=F32)
        return out

    for g in range(ATT_GROUP):
        s = jnp.where(visible, scores(g, qi, 1), -jnp.inf)
        m0 = jnp.max(s, axis=0, keepdims=True)
        m_scr[g] = m0
        acc_scr[g] = pv(g, qi, 1, jnp.exp2(s - m0).astype(BF16))

    def update(j, ntile):
        for g in range(ATT_GROUP):
            s = scores(g, j, ntile)
            m_old = m_scr[g]
            m_new = jnp.maximum(m_old, jnp.max(s, axis=0, keepdims=True))
            m_scr[g] = m_new
            acc_scr[g] = jnp.exp2(m_old - m_new) * acc_scr[g] + pv(g, j, ntile, jnp.exp2(s - m_new).astype(BF16))

    @pl.when(qi % 2 == 1)
    def _():
        update(0, 1)

    @pl.when((qi // 2) % 2 == 1)
    def _():
        update(qi % 2, 2)

    def body(t, carry):
        update(qi % 4 + 4 * t, 4)
        return carry

    lax.fori_loop(0, qi // 4, body, 0)

    lp = lam_ref[...]
    lam = (jnp.exp(jnp.sum(lp[0:1] * lp[1:2], axis=-1, keepdims=True))
           - jnp.exp(jnp.sum(lp[2:3] * lp[3:4], axis=-1, keepdims=True)) + lambda_init)
    for g in range(ATT_GROUP):
        acc = acc_scr[g]
        on = acc[0:LANES, :] / acc[LANES:LANES + 1, :]
        o = (on[:, 0:tq] - lam * on[:, tq:2 * tq]).T
        o_ref[0, :, g * LANES:(g + 1) * LANES] = (_rms(o, sw_ref[...]) * (1.0 - lambda_init)).astype(o_ref.dtype)


def diff_attention(qt, k, vt, lam_p, subln_w, lambda_init):
    b, t, _ = k.shape
    tq = ATT_TILE
    tiles = t // tq
    g = ATT_GROUP
    return pl.pallas_call(
        functools.partial(_attn_body, lambda_init=lambda_init),
        grid=(b, ATT_HEADS // g, tiles),
        in_specs=[pl.BlockSpec((1, g, 1, LANES, tq), lambda i, h, j: (i, h, j, 0, 0)),
                  pl.BlockSpec((1, t, g * LANES), lambda i, h, j: (i, 0, h)),
                  pl.BlockSpec((1, g, tiles, VT_ROWS, tq), lambda i, h, j: (i, h, 0, 0, 0)),
                  pl.BlockSpec((4, ATT_HEAD_DIM), lambda i, h, j: (0, 0)),
                  pl.BlockSpec((1, LANES), lambda i, h, j: (0, 0))],
        out_specs=pl.BlockSpec((1, tq, g * LANES), lambda i, h, j: (i, j, h)),
        out_shape=jax.ShapeDtypeStruct((b, t, ATT_DIM), BF16),
        scratch_shapes=[pltpu.VMEM((g, 1, 2 * tq), F32), pltpu.VMEM((g, VT_ROWS, 2 * tq), F32)],
        compiler_params=_params("parallel", "parallel", "arbitrary"),
        name="diff_attention",
    )(qt, k, vt, lam_p, subln_w.reshape(1, LANES))


def _matmul_res_body(a_ref, w_ref, h_ref, o_ref):
    o_ref[...] = h_ref[...] + jnp.dot(a_ref[...], w_ref[...], preferred_element_type=F32)


def matmul_residual(a_bf16, w_bf16, h):
    n, kdim = a_bf16.shape
    d = w_bf16.shape[1]
    return pl.pallas_call(
        _matmul_res_body,
        grid=(n // ROW_TILE,),
        in_specs=[pl.BlockSpec((ROW_TILE, kdim), lambda i: (i, 0)),
                  pl.BlockSpec((kdim, d), lambda i: (0, 0)),
                  pl.BlockSpec((ROW_TILE, d), lambda i: (i, 0))],
        out_specs=pl.BlockSpec((ROW_TILE, d), lambda i: (i, 0)),
        out_shape=jax.ShapeDtypeStruct((n, d), F32),
        compiler_params=_params("parallel"),
        name="matmul_residual",
    )(a_bf16, w_bf16, h)


def odd_layer(h, nw, w_qkv, lam_p, subln_w, w_out, lambda_init, tables):
    b, t, d = h.shape
    qt, k, vt = qkv_rope(h.reshape(b * t, d), nw, w_qkv.astype(BF16), tables, b, t)
    o = diff_attention(qt, k.reshape(b, t, ATT_DIM), vt, lam_p.astype(F32), subln_w.astype(F32), lambda_init)
    return matmul_residual(o.reshape(b * t, ATT_DIM), w_out.astype(BF16), h.reshape(b * t, d)).reshape(b, t, d)


META_E, META_R, META_G = 0, 2, 4


def _router_body(h_ref, nw_ref, wr_ref, br_ref, before_ref, meta_ref, cnt_ref, run_cnt):
    tm = MOE_TILE

    @pl.when(pl.program_id(0) == 0)
    def _():
        run_cnt[...] = jnp.zeros_like(run_cnt)

    xn = _rms(h_ref[...], nw_ref[...])
    x_hi = xn.astype(BF16)
    x_lo = (xn - x_hi.astype(F32)).astype(BF16)
    logits = (jnp.dot(x_hi, wr_ref[0], preferred_element_type=F32)
              + (jnp.dot(x_lo, wr_ref[0], preferred_element_type=F32)
                 + jnp.dot(x_hi, wr_ref[1], preferred_element_type=F32))) + br_ref[...]
    lane = lax.broadcasted_iota(jnp.int32, (tm, LANES), 1)
    big = jnp.int32(LANES)
    neg = -jnp.inf

    is_g = lane < MOE_GROUPS
    gl = jnp.where(is_g, logits, neg)
    gexp = jnp.exp(gl - jnp.max(gl, axis=-1, keepdims=True))
    gprob = gexp / jnp.sum(gexp, axis=-1, keepdims=True)
    g_gate = jnp.max(gprob, axis=-1, keepdims=True)
    g_idx = jnp.min(jnp.where(is_g & (gprob == g_gate), lane, big), axis=-1, keepdims=True)

    lo = ROUTER_LANE0 + g_idx * EXPERTS_PER_GROUP
    sel = (lane >= lo) & (lane < lo + EXPERTS_PER_GROUP)
    el = jnp.where(sel, logits, neg)
    eexp = jnp.exp(el - jnp.max(el, axis=-1, keepdims=True))
    eprob = jnp.where(sel, eexp / jnp.sum(eexp, axis=-1, keepdims=True), -1.0)
    p1 = jnp.max(eprob, axis=-1, keepdims=True)
    i1 = jnp.min(jnp.where(eprob == p1, lane, big), axis=-1, keepdims=True)
    rest = jnp.where(lane == i1, -1.0, eprob)
    p2 = jnp.max(rest, axis=-1, keepdims=True)
    i2 = jnp.min(jnp.where(rest == p2, lane, big), axis=-1, keepdims=True)
    g1 = g_gate * p1 / (p1 + p2)
    g2 = g_gate * p2 / (p1 + p2)

    hit1 = lane == i1
    hit2 = lane == i2
    assign = jnp.where(hit1 | hit2, 1.0, 0.0)
    prior = jnp.dot(before_ref[...], assign.astype(BF16), preferred_element_type=F32) + run_cnt[...]
    r1 = jnp.sum(jnp.where(hit1, prior, 0.0), axis=-1, keepdims=True)
    r2 = jnp.sum(jnp.where(hit2, prior, 0.0), axis=-1, keepdims=True)
    run_cnt[...] = run_cnt[...] + jnp.sum(assign, axis=0, keepdims=True)
    cnt_ref[...] = jnp.broadcast_to(run_cnt[...], cnt_ref.shape)

    e1 = (i1 - ROUTER_LANE0).astype(F32)
    e2 = (i2 - ROUTER_LANE0).astype(F32)
    rec = jnp.zeros((tm, LANES), F32)
    for pos, val in ((META_E, e1), (META_E + 1, e2), (META_R, r1), (META_R + 1, r2), (META_G, g1), (META_G + 1, g2)):
        rec = jnp.where(lane == pos, val, rec)
    meta_ref[...] = rec


def moe_router(h, nw, w_router, b_router):
    n, d = h.shape
    tm = MOE_TILE
    before = jnp.tril(jnp.ones((tm, tm), BF16), -1)
    return pl.pallas_call(
        _router_body,
        grid=(n // tm,),
        in_specs=[pl.BlockSpec((tm, d), lambda i: (i, 0)),
                  pl.BlockSpec((1, d), lambda i: (0, 0)),
                  pl.BlockSpec((2, d, LANES), lambda i: (0, 0, 0)),
                  pl.BlockSpec((1, LANES), lambda i: (0, 0)),
                  pl.BlockSpec((tm, tm), lambda i: (0, 0))],
        out_specs=[pl.BlockSpec((tm, LANES), lambda i: (i, 0)),
                   pl.BlockSpec((SUBLANES, LANES), lambda i: (0, 0))],
        out_shape=[jax.ShapeDtypeStruct((n, LANES), F32), jax.ShapeDtypeStruct((SUBLANES, LANES), F32)],
        scratch_shapes=[pltpu.VMEM((1, LANES), F32)],
        compiler_params=_params("arbitrary"),
        name="moe_router",
    )(h, nw.reshape(1, d), w_router, b_router, before)


def _row_copy(src, src_row, dst, dst_row, sem):
    return pltpu.make_async_copy(src.at[pl.ds(src_row, 1), :], dst.at[pl.ds(dst_row, 1), :], sem)


def _pack_rows(x):
    bits = lax.bitcast_convert_type(x.astype(BF16).astype(F32), jnp.uint32)
    return (bits[:, :PACKED] >> 16) | bits[:, PACKED:]


def _unpack_rows(u):
    lo = lax.bitcast_convert_type(u << 16, F32)
    hi = lax.bitcast_convert_type(u & jnp.uint32(0xFFFF0000), F32)
    return lo, hi


def _dispatch_body(pends_ref, padded_ref, dest_ref, h_ref, nw_ref, xrows_ref, xp_scr, zbuf, sem, zsem):
    tm = MOE_TILE

    @pl.when(pl.program_id(0) == 0)
    def _():
        zbuf[...] = jnp.zeros_like(zbuf)

        def tail(e):
            return xrows_ref.at[pl.ds(pl.multiple_of(pends_ref[e] - MOE_BLOCK, MOE_BLOCK), MOE_BLOCK), :]

        def fill(e, carry):
            @pl.when(padded_ref[e] > 0)
            def _():
                pltpu.make_async_copy(zbuf, tail(e), zsem).start()
            return carry

        def drain(e, carry):
            @pl.when(padded_ref[e] > 0)
            def _():
                pltpu.make_async_copy(zbuf, tail(e), zsem).wait()
            return carry

        def block(bk):
            return xrows_ref.at[pl.ds(pl.multiple_of(bk * MOE_BLOCK, MOE_BLOCK), MOE_BLOCK), :]

        def fill_block(bk, carry):
            pltpu.make_async_copy(zbuf, block(bk), zsem).start()
            return carry

        def drain_block(bk, carry):
            pltpu.make_async_copy(zbuf, block(bk), zsem).wait()
            return carry

        first_unused = pends_ref[N_EXPERTS - 1] // MOE_BLOCK
        n_blk = xrows_ref.shape[0] // MOE_BLOCK
        lax.fori_loop(0, N_EXPERTS, fill, 0)
        lax.fori_loop(first_unused, n_blk, fill_block, 0)
        lax.fori_loop(0, N_EXPERTS, drain, 0)
        lax.fori_loop(first_unused, n_blk, drain_block, 0)

    xp_scr[...] = _pack_rows(_rms(h_ref[...], nw_ref[...]))

    def issue(i, carry):
        _row_copy(xp_scr, i, xrows_ref, dest_ref[0, 0, 2 * i], sem).start()
        _row_copy(xp_scr, i, xrows_ref, dest_ref[0, 0, 2 * i + 1], sem).start()
        return carry

    lax.fori_loop(0, tm, issue, 0, unroll=DMA_UNROLL)
    for _ in range(2):
        pltpu.make_async_copy(xp_scr, xrows_ref.at[pl.ds(0, tm), :], sem).wait()


def moe_dispatch(h, nw, dest, pends, padded, n_rows):
    n, d = h.shape
    tm = MOE_TILE
    return pl.pallas_call(
        _dispatch_body,
        grid_spec=pltpu.PrefetchScalarGridSpec(
            num_scalar_prefetch=2,
            grid=(n // tm,),
            in_specs=[pl.BlockSpec((1, 1, 2 * tm), lambda i, pe, pa: (i, 0, 0), memory_space=pltpu.SMEM),
                      pl.BlockSpec((tm, d), lambda i, pe, pa: (i, 0)),
                      pl.BlockSpec((1, d), lambda i, pe, pa: (0, 0))],
            out_specs=pl.BlockSpec(memory_space=pl.ANY),
            scratch_shapes=[pltpu.VMEM((tm, PACKED), jnp.uint32), pltpu.VMEM((MOE_BLOCK, PACKED), jnp.uint32),
                            pltpu.SemaphoreType.DMA, pltpu.SemaphoreType.DMA]),
        out_shape=jax.ShapeDtypeStruct((n_rows, PACKED), jnp.uint32),
        compiler_params=_params("arbitrary"),
        name="moe_dispatch",
    )(pends, padded, dest.reshape(n // tm, 1, 2 * tm), h, nw.reshape(1, d))


def _ffn_body(blk_e_ref, n_used_ref, x_ref, w1_ref, w3_ref, w2_ref, y_ref, w1b, w3b, w2b):
    b = pl.program_id(0)
    prev = blk_e_ref[jnp.maximum(b - 1, 0)]
    active = b < n_used_ref[0]

    @pl.when(active & ((b == 0) | (blk_e_ref[b] != prev)))
    def _():
        w1b[...] = w1_ref[0, 0].astype(BF16)
        w3b[...] = w3_ref[0, 0].astype(BF16)
        w2b[...] = w2_ref[0, 0].astype(BF16)

    @pl.when(active)
    def _():
        lo, hi = _unpack_rows(x_ref[...])
        x = jnp.concatenate([lo.astype(BF16), hi.astype(BF16)], axis=1)
        u = jnp.dot(x, w1b[...], preferred_element_type=F32)
        g = jnp.dot(x, w3b[...], preferred_element_type=F32)
        y_ref[...] = _pack_rows(jnp.dot((_silu(u) * g).astype(BF16), w2b[...], preferred_element_type=F32))

    @pl.when(jnp.logical_not(active))
    def _():
        y_ref[...] = jnp.zeros_like(y_ref)


def moe_ffn(x_rows, blk_expert, n_used, w1, w3, w2, layer):
    n_rows = x_rows.shape[0]
    d, de = w1.shape[-2:]
    n_blk = n_rows // MOE_BLOCK
    row = lambda b, be, nu: (b, 0)
    row_in = lambda b, be, nu: (jnp.minimum(b, nu[0] - 1), 0)
    wsel = lambda b, be, nu: (layer, be[b], 0, 0)
    return pl.pallas_call(
        _ffn_body,
        grid_spec=pltpu.PrefetchScalarGridSpec(
            num_scalar_prefetch=2,
            grid=(n_blk,),
            in_specs=[pl.BlockSpec((MOE_BLOCK, PACKED), row_in),
                      pl.BlockSpec((1, 1, d, de), wsel),
                      pl.BlockSpec((1, 1, d, de), wsel),
                      pl.BlockSpec((1, 1, de, d), wsel)],
            out_specs=pl.BlockSpec((MOE_BLOCK, PACKED), row),
            scratch_shapes=[pltpu.VMEM((d, de), BF16), pltpu.VMEM((d, de), BF16), pltpu.VMEM((de, d), BF16)]),
        out_shape=jax.ShapeDtypeStruct((n_rows, PACKED), jnp.uint32),
        compiler_params=_params("arbitrary"),
        name="moe_ffn",
    )(blk_expert, n_used, x_rows, w1, w3, w2)


def _combine_body(dest_ref, meta_ref, h_ref, yrows_ref, fw_ref, o_ref, ybuf, sem, *, final_norm):
    tm = MOE_TILE

    def issue(i, carry):
        _row_copy(yrows_ref, dest_ref[0, 0, 2 * i], ybuf.at[0], i, sem).start()
        _row_copy(yrows_ref, dest_ref[0, 0, 2 * i + 1], ybuf.at[1], i, sem).start()
        return carry

    lax.fori_loop(0, tm, issue, 0, unroll=DMA_UNROLL)
    for half in range(2):
        pltpu.make_async_copy(yrows_ref.at[pl.ds(0, tm), :], ybuf.at[half], sem).wait()

    meta = meta_ref[...]
    g1 = meta[:, META_G:META_G + 1]
    g2 = meta[:, META_G + 1:META_G + 2]
    lo1, hi1 = _unpack_rows(ybuf[0])
    lo2, hi2 = _unpack_rows(ybuf[1])
    out = h_ref[...] + jnp.concatenate([g1 * lo1 + g2 * lo2, g1 * hi1 + g2 * hi2], axis=1)
    if final_norm:
        out = _rms(out, fw_ref[...])
    o_ref[...] = out


def moe_combine(h, meta, dest, y_rows, final_w):
    n, d = h.shape
    tm = MOE_TILE
    final_norm = final_w is not None
    fw = (final_w if final_norm else jnp.ones((d,), F32)).reshape(1, d)
    return pl.pallas_call(
        functools.partial(_combine_body, final_norm=final_norm),
        grid=(n // tm,),
        in_specs=[pl.BlockSpec((1, 1, 2 * tm), lambda i: (i, 0, 0), memory_space=pltpu.SMEM),
                  pl.BlockSpec((tm, LANES), lambda i: (i, 0)),
                  pl.BlockSpec((tm, d), lambda i: (i, 0)),
                  pl.BlockSpec(memory_space=pl.ANY),
                  pl.BlockSpec((1, d), lambda i: (0, 0))],
        out_specs=pl.BlockSpec((tm, d), lambda i: (i, 0)),
        out_shape=jax.ShapeDtypeStruct((n, d), F32),
        scratch_shapes=[pltpu.VMEM((2, tm, PACKED), jnp.uint32), pltpu.SemaphoreType.DMA],
        compiler_params=_params("arbitrary"),
        name="moe_combine",
    )(dest.reshape(n // tm, 1, 2 * tm), meta, h, y_rows, fw)


def hier_moe_layer(h, nw, w_group, b_group, w_expert, b_expert, w1, w3, w2, layer, final_w):
    n, d = h.shape
    pad_cols = LANES - MOE_GROUPS - N_EXPERTS
    w_router = jnp.concatenate([w_group, w_expert, jnp.zeros((d, pad_cols), F32)], axis=1)
    w_hi = w_router.astype(BF16)
    w_router = jnp.stack([w_hi, (w_router - w_hi.astype(F32)).astype(BF16)])
    b_router = jnp.concatenate([b_group, b_expert, jnp.zeros((pad_cols,), F32)]).reshape(1, LANES)
    meta, cnt = moe_router(h, nw, w_router, b_router)

    counts = cnt[0, ROUTER_LANE0:ROUTER_LANE0 + N_EXPERTS].astype(jnp.int32)
    padded = (counts + MOE_BLOCK - 1) // MOE_BLOCK * MOE_BLOCK
    pends = jnp.cumsum(padded)
    pstarts = pends - padded
    expert = meta[:, META_E:META_E + 2].astype(jnp.int32)
    rank = meta[:, META_R:META_R + 2].astype(jnp.int32)
    hit = expert[:, :, None] == jnp.arange(N_EXPERTS, dtype=jnp.int32)
    dest = (jnp.sum(jnp.where(hit, pstarts, 0), axis=-1) + rank).reshape(n * 2)
    n_rows = n * 2 + N_EXPERTS * MOE_BLOCK
    n_blk = n_rows // MOE_BLOCK
    blk_start = jnp.arange(n_blk, dtype=jnp.int32) * MOE_BLOCK
    blk_expert = jnp.minimum(jnp.sum((pends[None, :] <= blk_start[:, None]).astype(jnp.int32), axis=1),
                             N_EXPERTS - 1)
    n_used = (pends[-1:] // MOE_BLOCK).astype(jnp.int32)

    x_rows = moe_dispatch(h, nw, dest, pends.astype(jnp.int32), padded.astype(jnp.int32), n_rows)
    y_rows = moe_ffn(x_rows, blk_expert, n_used, w1, w3, w2, layer)
    return moe_combine(h, meta, dest, y_rows, final_w)


def _lambda_init(layer):
    return 0.8 - 0.6 * math.exp(-0.3 * layer)


def kernel(x, norm_mix_w, norm_ffn_w, final_norm_w, ev_w_in, ev_conv_w, ev_conv_b, ev_dt_bias, ev_a_log,
           ev_d_skip, ev_gnorm_w, ev_sconv_w, ev_w_out, od_w_qkv, od_lambda, od_subln_w, od_w_out,
           moe_w_group, moe_b_group, moe_w_expert, moe_b_expert, moe_w1, moe_w3, moe_w2):
    b, t, d = x.shape
    depth = norm_mix_w.shape[0]
    tables = _rope_tables(t)
    h = x
    for layer in range(depth):
        i = layer // 2
        if layer % 2 == 0:
            h = even_layer(h, norm_mix_w[layer], ev_w_in[i], ev_conv_w[i], ev_conv_b[i], ev_dt_bias[i],
                           ev_a_log[i], ev_d_skip[i], ev_gnorm_w[i], ev_sconv_w[i], ev_w_out[i])
        else:
            h = odd_layer(h, norm_mix_w[layer], od_w_qkv[i], od_lambda[i], od_subln_w[i], od_w_out[i],
                          _lambda_init(layer), tables)
        final_w = final_norm_w if layer == depth - 1 else None
        h = hier_moe_layer(h.reshape(b * t, d), norm_ffn_w[layer], moe_w_group[layer], moe_b_group[layer],
                           moe_w_expert[layer], moe_b_expert[layer], moe_w1, moe_w3, moe_w2, layer,
                           final_w).reshape(b, t, d)
    return h
```

```python
import functools
import math

import jax
import jax.numpy as jnp
from jax import lax
from jax.experimental import pallas as pl
from jax.experimental.pallas import tpu as pltpu

F32 = jnp.float32
BF16 = jnp.bfloat16

D_MODEL = 1024
CHUNK = 64
SSM_INNER = 1536
SSM_HEAD_DIM = 64
SSM_HEADS = 24
SSM_GROUPS = 4
SSM_STATE = 128
SSM_CONV = 4
SSM_CONV_CH = SSM_INNER + 2 * SSM_GROUPS * SSM_STATE
SSM_PAIRS = SSM_HEADS // 2
PAIRS_PER_GROUP = SSM_PAIRS // SSM_GROUPS
GROUP_CH = SSM_INNER // SSM_GROUPS
SCONV_DIM = 512
SCONV_WIDTH = 3
EVEN_MIX = SSM_INNER + SCONV_DIM
ATT_HEADS = 8
ATT_HEAD_DIM = 64
ATT_DIM = ATT_HEADS * 2 * ATT_HEAD_DIM
ROPE_DIM = ATT_HEAD_DIM // 4
ROPE_THETA = 500000.0
MOE_GROUPS = 8
EXPERTS_PER_GROUP = 8
N_EXPERTS = 64
D_EXPERT = 512
EPS = 1e-6

LANES = 128
SUBLANES = 8
VMEM_LIMIT = 56 * 1024 * 1024

OFF_Z = 0
OFF_XBC = SSM_INNER
OFF_HB = OFF_XBC + SSM_CONV_CH
OFF_BG = OFF_HB + SCONV_DIM
OFF_CG = OFF_BG + SCONV_DIM
OFF_DT = OFF_CG + SCONV_DIM
EVEN_PROJ = OFF_DT + LANES

ROW_TILE = 256
SSD_CHUNK = 256
ATT_TILE = 256
ATT_GROUP = 4
VT_ROWS = LANES + 16
LOG2E = 1.4426950408889634
ROUTER_TILE = 512
MOE_TILE = 1024
MOE_BLOCK = 256
PACKED = D_MODEL // 2
DMA_UNROLL = 8
ROUTER_LANE0 = MOE_GROUPS


def _params(*sem):
    return pltpu.CompilerParams(dimension_semantics=sem, vmem_limit_bytes=VMEM_LIMIT)


def _rms(x, w):
    return x * lax.rsqrt(jnp.mean(x * x, axis=-1, keepdims=True) + EPS) * w


def _silu(x):
    return x * (1.0 / (1.0 + jnp.exp(-x)))


def _norm_matmul_body(x_ref, nw_ref, w_ref, o_ref, *, col_chunk):
    xn = _rms(x_ref[...], nw_ref[...]).astype(BF16)
    for c0 in range(0, o_ref.shape[-1], col_chunk):
        o_ref[:, c0:c0 + col_chunk] = jnp.dot(
            xn, w_ref[:, c0:c0 + col_chunk], preferred_element_type=F32).astype(o_ref.dtype)


def norm_matmul(x, nw, w_bf16, col_chunk):
    n, d = x.shape
    f = w_bf16.shape[1]
    return pl.pallas_call(
        functools.partial(_norm_matmul_body, col_chunk=col_chunk),
        grid=(n // ROW_TILE,),
        in_specs=[pl.BlockSpec((ROW_TILE, d), lambda i: (i, 0)),
                  pl.BlockSpec((1, d), lambda i: (0, 0)),
                  pl.BlockSpec((d, f), lambda i: (0, 0))],
        out_specs=pl.BlockSpec((ROW_TILE, f), lambda i: (i, 0)),
        out_shape=jax.ShapeDtypeStruct((n, f), F32),
        compiler_params=_params("parallel"),
        name="norm_matmul",
    )(x, nw.reshape(1, d), w_bf16)


def _softplus(x):
    return jnp.maximum(x, 0.0) + jnp.log(1.0 + jnp.exp(-jnp.abs(x)))


def _ssd_body(proj_ref, h_ref, cw_ref, cb_ref, dtb_ref, aneg_ref, dsk_ref, gnw_ref, scw_ref, wout_ref,
              o_ref, cbuf, xact, mix, state, sbuf):
    tc = ROW_TILE
    halo = SUBLANES

    @pl.when(pl.program_id(1) == 0)
    def _():
        state[...] = jnp.zeros_like(state)
        cbuf[0:halo, :] = jnp.zeros((halo, SSM_CONV_CH), F32)
        sbuf[0:halo, :] = jnp.zeros((halo, SCONV_DIM), F32)

    cbuf[halo:halo + tc, :] = proj_ref[0, :, OFF_XBC:OFF_XBC + SSM_CONV_CH]
    for c0 in range(0, SSM_CONV_CH, 512):
        xin = cbuf[:, c0:c0 + 512]
        acc = cb_ref[:, c0:c0 + 512] + cw_ref[SSM_CONV - 1:SSM_CONV, c0:c0 + 512] * xin
        for back in range(1, SSM_CONV):
            j = SSM_CONV - 1 - back
            acc = acc + cw_ref[j:j + 1, c0:c0 + 512] * pltpu.roll(xin, back, 0)
        xact[:, c0:c0 + 512] = _silu(acc[halo:halo + tc])
    cbuf[0:halo, :] = cbuf[tc:tc + halo, :]

    sbuf[halo:halo + tc, :] = (proj_ref[0, :, OFF_CG:OFF_CG + SCONV_DIM]
                               * proj_ref[0, :, OFF_HB:OFF_HB + SCONV_DIM])
    sin = sbuf[...]
    conv = scw_ref[SCONV_WIDTH - 1:SCONV_WIDTH, :] * sin
    for back in range(1, SCONV_WIDTH):
        j = SCONV_WIDTH - 1 - back
        conv = conv + scw_ref[j:j + 1, :] * pltpu.roll(sin, back, 0)
    mix[:, SSM_INNER:EVEN_MIX] = (proj_ref[0, :, OFF_BG:OFF_BG + SCONV_DIM] * conv[halo:halo + tc]).astype(BF16)
    sbuf[0:halo, :] = sbuf[tc:tc + halo, :]

    L = SSD_CHUNK
    lane_p = lax.broadcasted_iota(jnp.int32, (L, LANES), 1)
    p_lo = lane_p < SSM_HEAD_DIM
    sub_s = lax.broadcasted_iota(jnp.int32, (L, 2 * L), 0)
    lane_s = lax.broadcasted_iota(jnp.int32, (L, 2 * L), 1)
    s_lo = lane_s < L
    tril2 = sub_s >= jnp.where(s_lo, lane_s, lane_s - L)
    csum_l = lax.broadcasted_iota(jnp.int32, (L, L), 0)
    csum_s = lax.broadcasted_iota(jnp.int32, (L, L), 1)
    tril_incl = (csum_s <= csum_l).astype(F32)
    sub2 = lax.broadcasted_iota(jnp.int32, (LANES, LANES), 0)
    rhs_top = lax.broadcasted_iota(jnp.int32, (2 * L, LANES), 0) < L
    rhs_lo = lax.broadcasted_iota(jnp.int32, (2 * L, LANES), 1) < SSM_HEAD_DIM

    def chunk_body(c, carry):
        r0 = pl.multiple_of(c * L, L)
        rows = pl.ds(r0, L)
        dt = _softplus(proj_ref[0, rows, OFF_DT:OFF_DT + LANES] + dtb_ref[...])
        a = dt * aneg_ref[...]
        acum = jnp.dot(tril_incl, a, preferred_element_type=F32, precision=lax.Precision.HIGHEST)
        acum_t = acum.T
        for g in range(SSM_GROUPS):
            bg = xact[rows, SSM_INNER + g * SSM_STATE:SSM_INNER + (g + 1) * SSM_STATE].astype(BF16)
            c_off = SSM_INNER + SSM_GROUPS * SSM_STATE
            cg = xact[rows, c_off + g * SSM_STATE:c_off + (g + 1) * SSM_STATE].astype(BF16)
            cb = lax.dot_general(cg, bg, (((1,), (1,)), ((), ())), preferred_element_type=F32)
            cb2 = jnp.concatenate([cb, cb], axis=1)
            ys = []
            for jj in range(PAIRS_PER_GROUP):
                j = g * PAIRS_PER_GROUP + jj
                h0, h1 = 2 * j, 2 * j + 1
                xp = xact[rows, j * LANES:(j + 1) * LANES]
                dcol0, dcol1 = dt[:, h0:h0 + 1], dt[:, h1:h1 + 1]
                acol0, acol1 = acum[:, h0:h0 + 1], acum[:, h1:h1 + 1]
                dt_p = jnp.where(p_lo, dcol0, dcol1)
                col_p = jnp.where(p_lo, acol0, acol1)
                col_s = jnp.where(s_lo, acol0, acol1)
                row_s = jnp.concatenate([acum_t[h0:h0 + 1, :], acum_t[h1:h1 + 1, :]], axis=1)
                last_p = col_p[L - 1:L, :]
                xdt = xp * dt_p
                m2 = cb2 * jnp.exp(jnp.where(tril2, col_s - row_s, -jnp.inf))
                xdt2 = jnp.concatenate([xdt, xdt], axis=0)
                rhs = jnp.where(rhs_top == rhs_lo, xdt2, 0.0).astype(BF16)
                y2 = jnp.dot(m2.astype(BF16), rhs, preferred_element_type=F32)
                s_old = state[j]
                y_in = lax.dot_general(cg, s_old.astype(BF16), (((1,), (1,)), ((), ())),
                                       preferred_element_type=F32)
                y2 = y2 + y_in * jnp.exp(col_p) + dsk_ref[:, j * LANES:(j + 1) * LANES] * xp
                xd = (xdt * jnp.exp(last_p - col_p)).astype(BF16)
                upd = lax.dot_general(xd, bg, (((0,), (0,)), ((), ())), preferred_element_type=F32)
                e0 = jnp.exp(acum_t[h0:h0 + 1, L - 1:L])
                e1 = jnp.exp(acum_t[h1:h1 + 1, L - 1:L])
                state[j] = s_old * jnp.where(sub2 < SSM_HEAD_DIM, e0, e1) + upd
                ys.append(y2)
            yg = jnp.concatenate(ys, axis=1)
            yg = yg * _silu(proj_ref[0, rows, OFF_Z + g * GROUP_CH:OFF_Z + (g + 1) * GROUP_CH])
            mix[rows, g * GROUP_CH:(g + 1) * GROUP_CH] = _rms(
                yg, gnw_ref[:, g * GROUP_CH:(g + 1) * GROUP_CH]).astype(BF16)
        return carry

    lax.fori_loop(0, tc // SSD_CHUNK, chunk_body, 0)
    o_ref[0] = h_ref[0] + jnp.dot(mix[...], wout_ref[...], preferred_element_type=F32)


def ssd_mixer(proj, h, cw, cb, dtb, aneg, dsk, gnw, scw, wout_bf16):
    b, t, _ = h.shape
    tc = ROW_TILE
    const = lambda shape: pl.BlockSpec(shape, lambda i, j: (0,) * len(shape))
    return pl.pallas_call(
        _ssd_body,
        grid=(b, t // tc),
        in_specs=[pl.BlockSpec((1, tc, EVEN_PROJ), lambda i, j: (i, j, 0)),
                  pl.BlockSpec((1, tc, D_MODEL), lambda i, j: (i, j, 0)),
                  const((SSM_CONV, SSM_CONV_CH)), const((1, SSM_CONV_CH)),
                  const((1, LANES)), const((1, LANES)), const((1, SSM_INNER)), const((1, SSM_INNER)),
                  const((SCONV_WIDTH, SCONV_DIM)), const((EVEN_MIX, D_MODEL))],
        out_specs=pl.BlockSpec((1, tc, D_MODEL), lambda i, j: (i, j, 0)),
        out_shape=jax.ShapeDtypeStruct((b, t, D_MODEL), F32),
        scratch_shapes=[pltpu.VMEM((tc + SUBLANES, SSM_CONV_CH), F32),
                        pltpu.VMEM((tc, SSM_CONV_CH), F32),
                        pltpu.VMEM((tc, EVEN_MIX), BF16),
                        pltpu.VMEM((SSM_PAIRS, LANES, SSM_STATE), F32),
                        pltpu.VMEM((tc + SUBLANES, SCONV_DIM), F32)],
        compiler_params=_params("parallel", "arbitrary"),
        name="ssd_mixer",
    )(proj, h, cw, cb, dtb, aneg, dsk, gnw, scw, wout_bf16)


def even_layer(h, nw, w_in, conv_w, conv_b, dt_bias, a_log, d_skip, gnorm_w, sconv_w, w_out):
    b, t, d = h.shape
    o1 = SSM_INNER
    o2 = o1 + SSM_CONV_CH
    o3 = o2 + SSM_HEADS
    o4 = o3 + SCONV_DIM
    o5 = o4 + SCONV_DIM
    pad = jnp.zeros((d, LANES - SSM_HEADS), F32)
    w_re = jnp.concatenate([w_in[:, :o2], w_in[:, o3:], w_in[:, o2:o3], pad], axis=1).astype(BF16)
    proj = norm_matmul(h.reshape(b * t, d), nw, w_re, 512).reshape(b, t, EVEN_PROJ)
    lane_pad = lambda v: jnp.concatenate([v.astype(F32), jnp.zeros((LANES - SSM_HEADS,), F32)]).reshape(1, LANES)
    aneg = lane_pad(-jnp.exp(a_log.astype(F32)))
    dsk = jnp.repeat(d_skip.astype(F32), SSM_HEAD_DIM).reshape(1, SSM_INNER)
    del o4, o5
    return ssd_mixer(proj, h, conv_w, conv_b.reshape(1, -1), lane_pad(dt_bias), aneg, dsk,
                     gnorm_w.reshape(1, -1), sconv_w, w_out.astype(BF16))


def _qkv_body(x_ref, nw_ref, w_ref, cos_ref, sa_ref, sb_ref, qt_ref, k_ref, vt_ref):
    xn = _rms(x_ref[...], nw_ref[...]).astype(BF16)
    cos, sa, sb = cos_ref[...], sa_ref[...], sb_ref[...]

    def rope(a):
        return a * cos + pltpu.roll(a, ROPE_DIM // 2, 1) * sa + pltpu.roll(a, LANES - ROPE_DIM // 2, 1) * sb

    for c2 in range(0, ATT_HEADS, 2):
        proj = lambda part: jnp.dot(xn, w_ref[:, part * ATT_DIM + c2 * LANES:part * ATT_DIM + (c2 + 2) * LANES],
                                    preferred_element_type=F32)
        q2, k2, v2 = proj(0), proj(1), proj(2)
        for i in range(2):
            c = c2 + i
            lanes = slice(i * LANES, (i + 1) * LANES)
            qt_ref[0, c, 0] = (rope(q2[:, lanes]) * (ATT_HEAD_DIM ** -0.5 * LOG2E)).astype(BF16).T
            k_ref[:, c * LANES:(c + 1) * LANES] = rope(k2[:, lanes]).astype(BF16)
            vt_ref[0, c, 0, 0:LANES, :] = v2[:, lanes].astype(BF16).T
            row = lax.broadcasted_iota(jnp.int32, (VT_ROWS - LANES, ROW_TILE), 0)
            vt_ref[0, c, 0, LANES:VT_ROWS, :] = jnp.where(row == 0, 1.0, 0.0).astype(BF16)


def _rope_tables(t):
    half = ROPE_DIM // 2
    inv = 1.0 / (ROPE_THETA ** (jnp.arange(0, ROPE_DIM, 2, dtype=F32) / ROPE_DIM))
    ang = jnp.arange(t, dtype=F32)[:, None] * inv[None, :]
    cos, sin = jnp.cos(ang), jnp.sin(ang)
    rest = ATT_HEAD_DIM - ROPE_DIM
    one = jnp.ones((t, rest), F32)
    zero = jnp.zeros((t, rest), F32)
    zh = jnp.zeros((t, half), F32)
    cos64 = jnp.concatenate([cos, cos, one], axis=1)
    sa64 = jnp.concatenate([zh, sin, zero], axis=1)
    sb64 = jnp.concatenate([-sin, zh, zero], axis=1)
    tile2 = lambda v: jnp.concatenate([v, v], axis=1)
    return tile2(cos64), tile2(sa64), tile2(sb64)


def qkv_rope(x, nw, w_bf16, tables, b, t):
    n, d = x.shape
    assert ROW_TILE == ATT_TILE
    tiles = t // ROW_TILE
    tab = pl.BlockSpec((ROW_TILE, LANES), lambda i: (i % tiles, 0))
    tr = lambda rows: pl.BlockSpec((1, ATT_HEADS, 1, rows, ROW_TILE), lambda i: (i // tiles, 0, i % tiles, 0, 0))
    tr_shape = lambda rows: jax.ShapeDtypeStruct((b, ATT_HEADS, tiles, rows, ROW_TILE), BF16)
    return pl.pallas_call(
        _qkv_body,
        grid=(n // ROW_TILE,),
        in_specs=[pl.BlockSpec((ROW_TILE, d), lambda i: (i, 0)),
                  pl.BlockSpec((1, d), lambda i: (0, 0)),
                  pl.BlockSpec((d, 3 * ATT_DIM), lambda i: (0, 0)),
                  tab, tab, tab],
        out_specs=[tr(LANES), pl.BlockSpec((ROW_TILE, ATT_DIM), lambda i: (i, 0)), tr(VT_ROWS)],
        out_shape=[tr_shape(LANES), jax.ShapeDtypeStruct((n, ATT_DIM), BF16), tr_shape(VT_ROWS)],
        compiler_params=_params("parallel"),
        name="qkv_rope",
    )(x, nw.reshape(1, d), w_bf16, *tables)


def _attn_body(qt_ref, k_ref, vt_ref, lam_ref, sw_ref, o_ref, m_scr, acc_scr, *, lambda_init):
    tq = ATT_TILE
    qi = pl.program_id(2)
    feat = lax.broadcasted_iota(jnp.int32, (LANES, tq), 0)
    rk = lax.broadcasted_iota(jnp.int32, (tq, 2 * tq), 0)
    rq = lax.broadcasted_iota(jnp.int32, (tq, 2 * tq), 1)
    rq = jnp.where(rq >= tq, rq - tq, rq)
    visible = (rk // CHUNK) <= (rq // CHUNK)

    qs = []
    for g in range(ATT_GROUP):
        qt = qt_ref[0, g, 0]
        zero = jnp.zeros_like(qt)
        qs.append(jnp.concatenate([jnp.where(feat < ATT_HEAD_DIM, qt, zero),
                                   jnp.where(feat < ATT_HEAD_DIM, zero, qt)], axis=1))

    def scores(g, j, ntile):
        off = pl.multiple_of(j * tq, tq)
        return jnp.dot(k_ref[0, pl.ds(off, ntile * tq), g * LANES:(g + 1) * LANES], qs[g],
                       preferred_element_type=F32)

    def pv(g, j, ntile, p):
        out = jnp.dot(vt_ref[0, g, j], p[0:tq], preferred_element_type=F32)
        for i in range(1, ntile):
            out = out + jnp.dot(vt_ref[0, g, j + i], p[i * tq:(i + 1) * tq], preferred_element_type=F32)
        return out

    for g in range(ATT_GROUP):
        s = jnp.where(visible, scores(g, qi, 1), -jnp.inf)
        m0 = jnp.max(s, axis=0, keepdims=True)
        m_scr[g] = m0
        acc_scr[g] = pv(g, qi, 1, jnp.exp2(s - m0).astype(BF16))

    def update(j, ntile):
        for g in range(ATT_GROUP):
            s = scores(g, j, ntile)
            m_old = m_scr[g]
            m_new = jnp.maximum(m_old, jnp.max(s, axis=0, keepdims=True))
            m_scr[g] = m_new
            acc_scr[g] = jnp.exp2(m_old - m_new) * acc_scr[g] + pv(g, j, ntile, jnp.exp2(s - m_new).astype(BF16))

    @pl.when(qi % 2 == 1)
    def _():
        update(0, 1)

    @pl.when((qi // 2) % 2 == 1)
    def _():
        update(qi % 2, 2)

    def body(t, carry):
        update(qi % 4 + 4 * t, 4)
        return carry

    lax.fori_loop(0, qi // 4, body, 0)

    lp = lam_ref[...]
    lam = (jnp.exp(jnp.sum(lp[0:1] * lp[1:2], axis=-1, keepdims=True))
           - jnp.exp(jnp.sum(lp[2:3] * lp[3:4], axis=-1, keepdims=True)) + lambda_init)
    for g in range(ATT_GROUP):
        acc = acc_scr[g]
        on = acc[0:LANES, :] / acc[LANES:LANES + 1, :]
        o = (on[:, 0:tq] - lam * on[:, tq:2 * tq]).T
        o_ref[0, :, g * LANES:(g + 1) * LANES] = (_rms(o, sw_ref[...]) * (1.0 - lambda_init)).astype(o_ref.dtype)


def diff_attention(qt, k, vt, lam_p, subln_w, lambda_init):
    b, t, _ = k.shape
    tq = ATT_TILE
    tiles = t // tq
    g = ATT_GROUP
    return pl.pallas_call(
        functools.partial(_attn_body, lambda_init=lambda_init),
        grid=(b, ATT_HEADS // g, tiles),
        in_specs=[pl.BlockSpec((1, g, 1, LANES, tq), lambda i, h, j: (i, h, j, 0, 0)),
                  pl.BlockSpec((1, t, g * LANES), lambda i, h, j: (i, 0, h)),
                  pl.BlockSpec((1, g, tiles, VT_ROWS, tq), lambda i, h, j: (i, h, 0, 0, 0)),
                  pl.BlockSpec((4, ATT_HEAD_DIM), lambda i, h, j: (0, 0)),
                  pl.BlockSpec((1, LANES), lambda i, h, j: (0, 0))],
        out_specs=pl.BlockSpec((1, tq, g * LANES), lambda i, h, j: (i, j, h)),
        out_shape=jax.ShapeDtypeStruct((b, t, ATT_DIM), BF16),
        scratch_shapes=[pltpu.VMEM((g, 1, 2 * tq), F32), pltpu.VMEM((g, VT_ROWS, 2 * tq), F32)],
        compiler_params=_params("parallel", "parallel", "arbitrary"),
        name="diff_attention",
    )(qt, k, vt, lam_p, subln_w.reshape(1, LANES))


def _matmul_res_body(a_ref, w_ref, h_ref, o_ref):
    o_ref[...] = h_ref[...] + jnp.dot(a_ref[...], w_ref[...], preferred_element_type=F32)


def matmul_residual(a_bf16, w_bf16, h):
    n, kdim = a_bf16.shape
    d = w_bf16.shape[1]
    return pl.pallas_call(
        _matmul_res_body,
        grid=(n // ROW_TILE,),
        in_specs=[pl.BlockSpec((ROW_TILE, kdim), lambda i: (i, 0)),
                  pl.BlockSpec((kdim, d), lambda i: (0, 0)),
                  pl.BlockSpec((ROW_TILE, d), lambda i: (i, 0))],
        out_specs=pl.BlockSpec((ROW_TILE, d), lambda i: (i, 0)),
        out_shape=jax.ShapeDtypeStruct((n, d), F32),
        compiler_params=_params("parallel"),
        name="matmul_residual",
    )(a_bf16, w_bf16, h)


def odd_layer(h, nw, w_qkv, lam_p, subln_w, w_out, lambda_init, tables):
    b, t, d = h.shape
    qt, k, vt = qkv_rope(h.reshape(b * t, d), nw, w_qkv.astype(BF16), tables, b, t)
    o = diff_attention(qt, k.reshape(b, t, ATT_DIM), vt, lam_p.astype(F32), subln_w.astype(F32), lambda_init)
    return matmul_residual(o.reshape(b * t, ATT_DIM), w_out.astype(BF16), h.reshape(b * t, d)).reshape(b, t, d)


META_E, META_R, META_G = 0, 2, 4


def _router_body(h_ref, nw_ref, wr_ref, br_ref, before_ref, meta_ref, cnt_ref, run_cnt):
    tm = ROUTER_TILE

    @pl.when(pl.program_id(0) == 0)
    def _():
        run_cnt[...] = jnp.zeros_like(run_cnt)

    xn = _rms(h_ref[...], nw_ref[...])
    x_hi = xn.astype(BF16)
    x_lo = (xn - x_hi.astype(F32)).astype(BF16)
    logits = (jnp.dot(x_hi, wr_ref[0], preferred_element_type=F32)
              + (jnp.dot(x_lo, wr_ref[0], preferred_element_type=F32)
                 + jnp.dot(x_hi, wr_ref[1], preferred_element_type=F32))) + br_ref[...]
    lane = lax.broadcasted_iota(jnp.int32, (tm, LANES), 1)
    big = jnp.int32(LANES)
    neg = -jnp.inf

    is_g = lane < MOE_GROUPS
    gl = jnp.where(is_g, logits, neg)
    gexp = jnp.exp(gl - jnp.max(gl, axis=-1, keepdims=True))
    gprob = gexp / jnp.sum(gexp, axis=-1, keepdims=True)
    g_gate = jnp.max(gprob, axis=-1, keepdims=True)
    g_idx = jnp.min(jnp.where(is_g & (gprob == g_gate), lane, big), axis=-1, keepdims=True)

    lo = ROUTER_LANE0 + g_idx * EXPERTS_PER_GROUP
    sel = (lane >= lo) & (lane < lo + EXPERTS_PER_GROUP)
    el = jnp.where(sel, logits, neg)
    eexp = jnp.exp(el - jnp.max(el, axis=-1, keepdims=True))
    eprob = jnp.where(sel, eexp / jnp.sum(eexp, axis=-1, keepdims=True), -1.0)
    p1 = jnp.max(eprob, axis=-1, keepdims=True)
    i1 = jnp.min(jnp.where(eprob == p1, lane, big), axis=-1, keepdims=True)
    rest = jnp.where(lane == i1, -1.0, eprob)
    p2 = jnp.max(rest, axis=-1, keepdims=True)
    i2 = jnp.min(jnp.where(rest == p2, lane, big), axis=-1, keepdims=True)
    g1 = g_gate * p1 / (p1 + p2)
    g2 = g_gate * p2 / (p1 + p2)

    hit1 = lane == i1
    hit2 = lane == i2
    assign = jnp.where(hit1 | hit2, 1.0, 0.0)
    prior = jnp.dot(before_ref[...], assign.astype(BF16), preferred_element_type=F32) + run_cnt[...]
    r1 = jnp.sum(jnp.where(hit1, prior, 0.0), axis=-1, keepdims=True)
    r2 = jnp.sum(jnp.where(hit2, prior, 0.0), axis=-1, keepdims=True)
    run_cnt[...] = run_cnt[...] + jnp.sum(assign, axis=0, keepdims=True)
    cnt_ref[...] = jnp.broadcast_to(run_cnt[...], cnt_ref.shape)

    e1 = (i1 - ROUTER_LANE0).astype(F32)
    e2 = (i2 - ROUTER_LANE0).astype(F32)
    rec = jnp.zeros((tm, LANES), F32)
    for pos, val in ((META_E, e1), (META_E + 1, e2), (META_R, r1), (META_R + 1, r2), (META_G, g1), (META_G + 1, g2)):
        rec = jnp.where(lane == pos, val, rec)
    meta_ref[...] = rec


def moe_router(h, nw, w_router, b_router):
    n, d = h.shape
    tm = ROUTER_TILE
    before = jnp.tril(jnp.ones((tm, tm), BF16), -1)
    return pl.pallas_call(
        _router_body,
        grid=(n // tm,),
        in_specs=[pl.BlockSpec((tm, d), lambda i: (i, 0)),
                  pl.BlockSpec((1, d), lambda i: (0, 0)),
                  pl.BlockSpec((2, d, LANES), lambda i: (0, 0, 0)),
                  pl.BlockSpec((1, LANES), lambda i: (0, 0)),
                  pl.BlockSpec((tm, tm), lambda i: (0, 0))],
        out_specs=[pl.BlockSpec((tm, LANES), lambda i: (i, 0)),
                   pl.BlockSpec((SUBLANES, LANES), lambda i: (0, 0))],
        out_shape=[jax.ShapeDtypeStruct((n, LANES), F32), jax.ShapeDtypeStruct((SUBLANES, LANES), F32)],
        scratch_shapes=[pltpu.VMEM((1, LANES), F32)],
        compiler_params=_params("arbitrary"),
        name="moe_router",
    )(h, nw.reshape(1, d), w_router, b_router, before)


def _row_copy(src, src_row, dst, dst_row, sem):
    return pltpu.make_async_copy(src.at[pl.ds(src_row, 1), :], dst.at[pl.ds(dst_row, 1), :], sem)


def _pack_rows(x):
    bits = lax.bitcast_convert_type(x.astype(BF16).astype(F32), jnp.uint32)
    return (bits[:, :PACKED] >> 16) | bits[:, PACKED:]


def _unpack_rows(u):
    lo = lax.bitcast_convert_type(u << 16, F32)
    hi = lax.bitcast_convert_type(u & jnp.uint32(0xFFFF0000), F32)
    return lo, hi


def _dispatch_body(pends_ref, padded_ref, dest_ref, h_ref, nw_ref, xrows_ref, xp_scr, zbuf, sem, zsem):
    tm = MOE_TILE

    @pl.when(pl.program_id(0) == 0)
    def _():
        zbuf[...] = jnp.zeros_like(zbuf)

        def tail(e):
            return xrows_ref.at[pl.ds(pl.multiple_of(pends_ref[e] - MOE_BLOCK, MOE_BLOCK), MOE_BLOCK), :]

        def fill(e, carry):
            @pl.when(padded_ref[e] > 0)
            def _():
                pltpu.make_async_copy(zbuf, tail(e), zsem).start()
            return carry

        def drain(e, carry):
            @pl.when(padded_ref[e] > 0)
            def _():
                pltpu.make_async_copy(zbuf, tail(e), zsem).wait()
            return carry

        def block(bk):
            return xrows_ref.at[pl.ds(pl.multiple_of(bk * MOE_BLOCK, MOE_BLOCK), MOE_BLOCK), :]

        def fill_block(bk, carry):
            pltpu.make_async_copy(zbuf, block(bk), zsem).start()
            return carry

        def drain_block(bk, carry):
            pltpu.make_async_copy(zbuf, block(bk), zsem).wait()
            return carry

        first_unused = pends_ref[N_EXPERTS - 1] // MOE_BLOCK
        n_blk = xrows_ref.shape[0] // MOE_BLOCK
        lax.fori_loop(0, N_EXPERTS, fill, 0)
        lax.fori_loop(first_unused, n_blk, fill_block, 0)
        lax.fori_loop(0, N_EXPERTS, drain, 0)
        lax.fori_loop(first_unused, n_blk, drain_block, 0)

    xp_scr[...] = _pack_rows(_rms(h_ref[...], nw_ref[...]))

    def issue(i, carry):
        _row_copy(xp_scr, i, xrows_ref, dest_ref[0, 0, 2 * i], sem).start()
        _row_copy(xp_scr, i, xrows_ref, dest_ref[0, 0, 2 * i + 1], sem).start()
        return carry

    lax.fori_loop(0, tm, issue, 0, unroll=DMA_UNROLL)
    for _ in range(2):
        pltpu.make_async_copy(xp_scr, xrows_ref.at[pl.ds(0, tm), :], sem).wait()


def moe_dispatch(h, nw, dest, pends, padded, n_rows):
    n, d = h.shape
    tm = MOE_TILE
    return pl.pallas_call(
        _dispatch_body,
        grid_spec=pltpu.PrefetchScalarGridSpec(
            num_scalar_prefetch=2,
            grid=(n // tm,),
            in_specs=[pl.BlockSpec((1, 1, 2 * tm), lambda i, pe, pa: (i, 0, 0), memory_space=pltpu.SMEM),
                      pl.BlockSpec((tm, d), lambda i, pe, pa: (i, 0)),
                      pl.BlockSpec((1, d), lambda i, pe, pa: (0, 0))],
            out_specs=pl.BlockSpec(memory_space=pl.ANY),
            scratch_shapes=[pltpu.VMEM((tm, PACKED), jnp.uint32), pltpu.VMEM((MOE_BLOCK, PACKED), jnp.uint32),
                            pltpu.SemaphoreType.DMA, pltpu.SemaphoreType.DMA]),
        out_shape=jax.ShapeDtypeStruct((n_rows, PACKED), jnp.uint32),
        compiler_params=_params("arbitrary"),
        name="moe_dispatch",
    )(pends, padded, dest.reshape(n // tm, 1, 2 * tm), h, nw.reshape(1, d))


def _ffn_body(pstart_ref, nblk_ref, w1_ref, w3_ref, w2_ref, xrows_ref, yrows_ref,
              w1b, w3b, w2b, xbuf, ybuf, xsem, ysem):
    e = pl.program_id(0)
    nb = nblk_ref[e]
    base = pstart_ref[e]

    def rows(i):
        return pl.ds(pl.multiple_of(base + i * MOE_BLOCK, MOE_BLOCK), MOE_BLOCK)

    def x_copy(i, slot):
        return pltpu.make_async_copy(xrows_ref.at[rows(i), :], xbuf.at[slot], xsem.at[slot])

    def y_copy(i, slot):
        return pltpu.make_async_copy(ybuf.at[slot], yrows_ref.at[rows(i), :], ysem.at[slot])

    @pl.when(nb > 0)
    def _():
        x_copy(0, 0).start()
        w1b[...] = w1_ref[0, 0].astype(BF16)
        w3b[...] = w3_ref[0, 0].astype(BF16)
        w2b[...] = w2_ref[0, 0].astype(BF16)

        def block(i, carry):
            slot = i % 2
            x_copy(i, slot).wait()

            @pl.when(i + 1 < nb)
            def _():
                x_copy(i + 1, 1 - slot).start()

            @pl.when(i >= 2)
            def _():
                y_copy(i - 2, slot).wait()

            lo, hi = _unpack_rows(xbuf[slot])
            x = jnp.concatenate([lo.astype(BF16), hi.astype(BF16)], axis=1)
            u = jnp.dot(x, w1b[...], preferred_element_type=F32)
            g = jnp.dot(x, w3b[...], preferred_element_type=F32)
            ybuf[slot] = _pack_rows(jnp.dot((_silu(u) * g).astype(BF16), w2b[...], preferred_element_type=F32))
            y_copy(i, slot).start()
            return carry

        lax.fori_loop(0, nb, block, 0)

        @pl.when(nb >= 2)
        def _():
            y_copy(nb - 2, nb % 2).wait()

        y_copy(nb - 1, (nb - 1) % 2).wait()

    @pl.when(e == N_EXPERTS - 1)
    def _():
        ybuf[0] = jnp.zeros((MOE_BLOCK, PACKED), jnp.uint32)
        first_unused = (base + nb * MOE_BLOCK) // MOE_BLOCK
        n_blk = yrows_ref.shape[0] // MOE_BLOCK

        def tail(bk):
            return yrows_ref.at[pl.ds(pl.multiple_of(bk * MOE_BLOCK, MOE_BLOCK), MOE_BLOCK), :]

        def fill(bk, carry):
            pltpu.make_async_copy(ybuf.at[0], tail(bk), ysem.at[0]).start()
            return carry

        def drain(bk, carry):
            pltpu.make_async_copy(ybuf.at[0], tail(bk), ysem.at[0]).wait()
            return carry

        lax.fori_loop(first_unused, n_blk, fill, 0)
        lax.fori_loop(first_unused, n_blk, drain, 0)


def moe_ffn(x_rows, pstarts, nblk, w1, w3, w2, layer):
    n_rows = x_rows.shape[0]
    d, de = w1.shape[-2:]
    wsel = lambda e, ps, nb: (layer, e, 0, 0)
    return pl.pallas_call(
        _ffn_body,
        grid_spec=pltpu.PrefetchScalarGridSpec(
            num_scalar_prefetch=2,
            grid=(N_EXPERTS,),
            in_specs=[pl.BlockSpec((1, 1, d, de), wsel),
                      pl.BlockSpec((1, 1, d, de), wsel),
                      pl.BlockSpec((1, 1, de, d), wsel),
                      pl.BlockSpec(memory_space=pl.ANY)],
            out_specs=pl.BlockSpec(memory_space=pl.ANY),
            scratch_shapes=[pltpu.VMEM((d, de), BF16), pltpu.VMEM((d, de), BF16), pltpu.VMEM((de, d), BF16),
                            pltpu.VMEM((2, MOE_BLOCK, PACKED), jnp.uint32),
                            pltpu.VMEM((2, MOE_BLOCK, PACKED), jnp.uint32),
                            pltpu.SemaphoreType.DMA((2,)), pltpu.SemaphoreType.DMA((2,))]),
        out_shape=jax.ShapeDtypeStruct((n_rows, PACKED), jnp.uint32),
        compiler_params=_params("arbitrary"),
        name="moe_ffn",
    )(pstarts, nblk, w1, w3, w2, x_rows)


def _combine_body(dest_ref, meta_ref, h_ref, yrows_ref, fw_ref, o_ref, ybuf, sem, *, final_norm):
    tm = MOE_TILE

    def issue(i, carry):
        _row_copy(yrows_ref, dest_ref[0, 0, 2 * i], ybuf.at[0], i, sem).start()
        _row_copy(yrows_ref, dest_ref[0, 0, 2 * i + 1], ybuf.at[1], i, sem).start()
        return carry

    lax.fori_loop(0, tm, issue, 0, unroll=DMA_UNROLL)
    for half in range(2):
        pltpu.make_async_copy(yrows_ref.at[pl.ds(0, tm), :], ybuf.at[half], sem).wait()

    meta = meta_ref[...]
    g1 = meta[:, META_G:META_G + 1]
    g2 = meta[:, META_G + 1:META_G + 2]
    lo1, hi1 = _unpack_rows(ybuf[0])
    lo2, hi2 = _unpack_rows(ybuf[1])
    out = h_ref[...] + jnp.concatenate([g1 * lo1 + g2 * lo2, g1 * hi1 + g2 * hi2], axis=1)
    if final_norm:
        out = _rms(out, fw_ref[...])
    o_ref[...] = out


def moe_combine(h, meta, dest, y_rows, final_w):
    n, d = h.shape
    tm = MOE_TILE
    final_norm = final_w is not None
    fw = (final_w if final_norm else jnp.ones((d,), F32)).reshape(1, d)
    return pl.pallas_call(
        functools.partial(_combine_body, final_norm=final_norm),
        grid=(n // tm,),
        in_specs=[pl.BlockSpec((1, 1, 2 * tm), lambda i: (i, 0, 0), memory_space=pltpu.SMEM),
                  pl.BlockSpec((tm, LANES), lambda i: (i, 0)),
                  pl.BlockSpec((tm, d), lambda i: (i, 0)),
                  pl.BlockSpec(memory_space=pl.ANY),
                  pl.BlockSpec((1, d), lambda i: (0, 0))],
        out_specs=pl.BlockSpec((tm, d), lambda i: (i, 0)),
        out_shape=jax.ShapeDtypeStruct((n, d), F32),
        scratch_shapes=[pltpu.VMEM((2, tm, PACKED), jnp.uint32), pltpu.SemaphoreType.DMA],
        compiler_params=_params("arbitrary"),
        name="moe_combine",
    )(dest.reshape(n // tm, 1, 2 * tm), meta, h, y_rows, fw)


def hier_moe_layer(h, nw, w_group, b_group, w_expert, b_expert, w1, w3, w2, layer, final_w):
    n, d = h.shape
    pad_cols = LANES - MOE_GROUPS - N_EXPERTS
    w_router = jnp.concatenate([w_group, w_expert, jnp.zeros((d, pad_cols), F32)], axis=1)
    w_hi = w_router.astype(BF16)
    w_router = jnp.stack([w_hi, (w_router - w_hi.astype(F32)).astype(BF16)])
    b_router = jnp.concatenate([b_group, b_expert, jnp.zeros((pad_cols,), F32)]).reshape(1, LANES)
    meta, cnt = moe_router(h, nw, w_router, b_router)

    counts = cnt[0, ROUTER_LANE0:ROUTER_LANE0 + N_EXPERTS].astype(jnp.int32)
    padded = (counts + MOE_BLOCK - 1) // MOE_BLOCK * MOE_BLOCK
    pends = jnp.cumsum(padded)
    pstarts = pends - padded
    expert = meta[:, META_E:META_E + 2].astype(jnp.int32)
    rank = meta[:, META_R:META_R + 2].astype(jnp.int32)
    hit = expert[:, :, None] == jnp.arange(N_EXPERTS, dtype=jnp.int32)
    dest = (jnp.sum(jnp.where(hit, pstarts, 0), axis=-1) + rank).reshape(n * 2)
    n_rows = n * 2 + N_EXPERTS * MOE_BLOCK

    x_rows = moe_dispatch(h, nw, dest, pends.astype(jnp.int32), padded.astype(jnp.int32), n_rows)
    y_rows = moe_ffn(x_rows, pstarts.astype(jnp.int32), (padded // MOE_BLOCK).astype(jnp.int32), w1, w3, w2, layer)
    return moe_combine(h, meta, dest, y_rows, final_w)


def _lambda_init(layer):
    return 0.8 - 0.6 * math.exp(-0.3 * layer)


def kernel(x, norm_mix_w, norm_ffn_w, final_norm_w, ev_w_in, ev_conv_w, ev_conv_b, ev_dt_bias, ev_a_log,
           ev_d_skip, ev_gnorm_w, ev_sconv_w, ev_w_out, od_w_qkv, od_lambda, od_subln_w, od_w_out,
           moe_w_group, moe_b_group, moe_w_expert, moe_b_expert, moe_w1, moe_w3, moe_w2):
    b, t, d = x.shape
    depth = norm_mix_w.shape[0]
    tables = _rope_tables(t)
    h = x
    for layer in range(depth):
        i = layer // 2
        if layer % 2 == 0:
            h = even_layer(h, norm_mix_w[layer], ev_w_in[i], ev_conv_w[i], ev_conv_b[i], ev_dt_bias[i],
                           ev_a_log[i], ev_d_skip[i], ev_gnorm_w[i], ev_sconv_w[i], ev_w_out[i])
        else:
            h = odd_layer(h, norm_mix_w[layer], od_w_qkv[i], od_lambda[i], od_subln_w[i], od_w_out[i],
                          _lambda_init(layer), tables)
        final_w = final_norm_w if layer == depth - 1 else None
        h = hier_moe_layer(h.reshape(b * t, d), norm_ffn_w[layer], moe_w_group[layer], moe_b_group[layer],
                           moe_w_expert[layer], moe_b_expert[layer], moe_w1, moe_w3, moe_w2, layer,
                           final_w).reshape(b, t, d)
    return h
```

```python
import functools
import math

import jax
import jax.numpy as jnp
from jax import lax
from jax.experimental import pallas as pl
from jax.experimental.pallas import tpu as pltpu

F32 = jnp.float32
BF16 = jnp.bfloat16

D_MODEL = 1024
CHUNK = 64
SSM_INNER = 1536
SSM_HEAD_DIM = 64
SSM_HEADS = 24
SSM_GROUPS = 4
SSM_STATE = 128
SSM_CONV = 4
SSM_CONV_CH = SSM_INNER + 2 * SSM_GROUPS * SSM_STATE
SSM_PAIRS = SSM_HEADS // 2
PAIRS_PER_GROUP = SSM_PAIRS // SSM_GROUPS
GROUP_CH = SSM_INNER // SSM_GROUPS
SCONV_DIM = 512
SCONV_WIDTH = 3
EVEN_MIX = SSM_INNER + SCONV_DIM
ATT_HEADS = 8
ATT_HEAD_DIM = 64
ATT_DIM = ATT_HEADS * 2 * ATT_HEAD_DIM
ROPE_DIM = ATT_HEAD_DIM // 4
ROPE_THETA = 500000.0
MOE_GROUPS = 8
EXPERTS_PER_GROUP = 8
N_EXPERTS = 64
D_EXPERT = 512
EPS = 1e-6

LANES = 128
SUBLANES = 8
VMEM_LIMIT = 56 * 1024 * 1024

OFF_Z = 0
OFF_XBC = SSM_INNER
OFF_HB = OFF_XBC + SSM_CONV_CH
OFF_BG = OFF_HB + SCONV_DIM
OFF_CG = OFF_BG + SCONV_DIM
OFF_DT = OFF_CG + SCONV_DIM
EVEN_PROJ = OFF_DT + LANES

ROW_TILE = 256
SSD_CHUNK = 256
ATT_TILE = 256
ATT_GROUP = 4
VT_ROWS = LANES + 16
LOG2E = 1.4426950408889634
ROUTER_TILE = 512
MOE_TILE = 1024
MOE_BLOCK = 256
FFN_AHEAD = 2
PACKED = D_MODEL // 2
DMA_UNROLL = 8
ROUTER_LANE0 = MOE_GROUPS


def _params(*sem):
    return pltpu.CompilerParams(dimension_semantics=sem, vmem_limit_bytes=VMEM_LIMIT)


def _rms(x, w):
    return x * lax.rsqrt(jnp.mean(x * x, axis=-1, keepdims=True) + EPS) * w


def _silu(x):
    return x * (1.0 / (1.0 + jnp.exp(-x)))


def _norm_matmul_body(x_ref, nw_ref, w_ref, o_ref, *, col_chunk):
    xn = _rms(x_ref[...], nw_ref[...]).astype(BF16)
    for c0 in range(0, o_ref.shape[-1], col_chunk):
        o_ref[:, c0:c0 + col_chunk] = jnp.dot(
            xn, w_ref[:, c0:c0 + col_chunk], preferred_element_type=F32).astype(o_ref.dtype)


def norm_matmul(x, nw, w_bf16, col_chunk):
    n, d = x.shape
    f = w_bf16.shape[1]
    return pl.pallas_call(
        functools.partial(_norm_matmul_body, col_chunk=col_chunk),
        grid=(n // ROW_TILE,),
        in_specs=[pl.BlockSpec((ROW_TILE, d), lambda i: (i, 0)),
                  pl.BlockSpec((1, d), lambda i: (0, 0)),
                  pl.BlockSpec((d, f), lambda i: (0, 0))],
        out_specs=pl.BlockSpec((ROW_TILE, f), lambda i: (i, 0)),
        out_shape=jax.ShapeDtypeStruct((n, f), F32),
        compiler_params=_params("parallel"),
        name="norm_matmul",
    )(x, nw.reshape(1, d), w_bf16)


def _softplus(x):
    return jnp.maximum(x, 0.0) + jnp.log(1.0 + jnp.exp(-jnp.abs(x)))


def _ssd_body(proj_ref, h_ref, cw_ref, cb_ref, dtb_ref, aneg_ref, dsk_ref, gnw_ref, scw_ref, wout_ref,
              o_ref, cbuf, xact, mix, state, sbuf):
    tc = ROW_TILE
    halo = SUBLANES

    @pl.when(pl.program_id(1) == 0)
    def _():
        state[...] = jnp.zeros_like(state)
        cbuf[0:halo, :] = jnp.zeros((halo, SSM_CONV_CH), F32)
        sbuf[0:halo, :] = jnp.zeros((halo, SCONV_DIM), F32)

    cbuf[halo:halo + tc, :] = proj_ref[0, :, OFF_XBC:OFF_XBC + SSM_CONV_CH]
    for c0 in range(0, SSM_CONV_CH, 512):
        xin = cbuf[:, c0:c0 + 512]
        acc = cb_ref[:, c0:c0 + 512] + cw_ref[SSM_CONV - 1:SSM_CONV, c0:c0 + 512] * xin
        for back in range(1, SSM_CONV):
            j = SSM_CONV - 1 - back
            acc = acc + cw_ref[j:j + 1, c0:c0 + 512] * pltpu.roll(xin, back, 0)
        xact[:, c0:c0 + 512] = _silu(acc[halo:halo + tc])
    cbuf[0:halo, :] = cbuf[tc:tc + halo, :]

    sbuf[halo:halo + tc, :] = (proj_ref[0, :, OFF_CG:OFF_CG + SCONV_DIM]
                               * proj_ref[0, :, OFF_HB:OFF_HB + SCONV_DIM])
    sin = sbuf[...]
    conv = scw_ref[SCONV_WIDTH - 1:SCONV_WIDTH, :] * sin
    for back in range(1, SCONV_WIDTH):
        j = SCONV_WIDTH - 1 - back
        conv = conv + scw_ref[j:j + 1, :] * pltpu.roll(sin, back, 0)
    mix[:, SSM_INNER:EVEN_MIX] = (proj_ref[0, :, OFF_BG:OFF_BG + SCONV_DIM] * conv[halo:halo + tc]).astype(BF16)
    sbuf[0:halo, :] = sbuf[tc:tc + halo, :]

    L = SSD_CHUNK
    lane_p = lax.broadcasted_iota(jnp.int32, (L, LANES), 1)
    p_lo = lane_p < SSM_HEAD_DIM
    sub_s = lax.broadcasted_iota(jnp.int32, (L, 2 * L), 0)
    lane_s = lax.broadcasted_iota(jnp.int32, (L, 2 * L), 1)
    s_lo = lane_s < L
    tril2 = sub_s >= jnp.where(s_lo, lane_s, lane_s - L)
    csum_l = lax.broadcasted_iota(jnp.int32, (L, L), 0)
    csum_s = lax.broadcasted_iota(jnp.int32, (L, L), 1)
    tril_incl = (csum_s <= csum_l).astype(F32)
    sub2 = lax.broadcasted_iota(jnp.int32, (LANES, LANES), 0)
    rhs_top = lax.broadcasted_iota(jnp.int32, (2 * L, LANES), 0) < L
    rhs_lo = lax.broadcasted_iota(jnp.int32, (2 * L, LANES), 1) < SSM_HEAD_DIM

    def chunk_body(c, carry):
        r0 = pl.multiple_of(c * L, L)
        rows = pl.ds(r0, L)
        dt = _softplus(proj_ref[0, rows, OFF_DT:OFF_DT + LANES] + dtb_ref[...])
        a = dt * aneg_ref[...]
        acum = jnp.dot(tril_incl, a, preferred_element_type=F32, precision=lax.Precision.HIGHEST)
        acum_t = acum.T
        for g in range(SSM_GROUPS):
            bg = xact[rows, SSM_INNER + g * SSM_STATE:SSM_INNER + (g + 1) * SSM_STATE].astype(BF16)
            c_off = SSM_INNER + SSM_GROUPS * SSM_STATE
            cg = xact[rows, c_off + g * SSM_STATE:c_off + (g + 1) * SSM_STATE].astype(BF16)
            cb = lax.dot_general(cg, bg, (((1,), (1,)), ((), ())), preferred_element_type=F32)
            cb2 = jnp.concatenate([cb, cb], axis=1)
            ys = []
            for jj in range(PAIRS_PER_GROUP):
                j = g * PAIRS_PER_GROUP + jj
                h0, h1 = 2 * j, 2 * j + 1
                xp = xact[rows, j * LANES:(j + 1) * LANES]
                dcol0, dcol1 = dt[:, h0:h0 + 1], dt[:, h1:h1 + 1]
                acol0, acol1 = acum[:, h0:h0 + 1], acum[:, h1:h1 + 1]
                dt_p = jnp.where(p_lo, dcol0, dcol1)
                col_p = jnp.where(p_lo, acol0, acol1)
                col_s = jnp.where(s_lo, acol0, acol1)
                row_s = jnp.concatenate([acum_t[h0:h0 + 1, :], acum_t[h1:h1 + 1, :]], axis=1)
                last_p = col_p[L - 1:L, :]
                xdt = xp * dt_p
                m2 = cb2 * jnp.exp(jnp.where(tril2, col_s - row_s, -jnp.inf))
                xdt2 = jnp.concatenate([xdt, xdt], axis=0)
                rhs = jnp.where(rhs_top == rhs_lo, xdt2, 0.0).astype(BF16)
                y2 = jnp.dot(m2.astype(BF16), rhs, preferred_element_type=F32)
                s_old = state[j]
                y_in = lax.dot_general(cg, s_old.astype(BF16), (((1,), (1,)), ((), ())),
                                       preferred_element_type=F32)
                y2 = y2 + y_in * jnp.exp(col_p) + dsk_ref[:, j * LANES:(j + 1) * LANES] * xp
                xd = (xdt * jnp.exp(last_p - col_p)).astype(BF16)
                upd = lax.dot_general(xd, bg, (((0,), (0,)), ((), ())), preferred_element_type=F32)
                e0 = jnp.exp(acum_t[h0:h0 + 1, L - 1:L])
                e1 = jnp.exp(acum_t[h1:h1 + 1, L - 1:L])
                state[j] = s_old * jnp.where(sub2 < SSM_HEAD_DIM, e0, e1) + upd
                ys.append(y2)
            yg = jnp.concatenate(ys, axis=1)
            yg = yg * _silu(proj_ref[0, rows, OFF_Z + g * GROUP_CH:OFF_Z + (g + 1) * GROUP_CH])
            mix[rows, g * GROUP_CH:(g + 1) * GROUP_CH] = _rms(
                yg, gnw_ref[:, g * GROUP_CH:(g + 1) * GROUP_CH]).astype(BF16)
        return carry

    lax.fori_loop(0, tc // SSD_CHUNK, chunk_body, 0)
    o_ref[0] = h_ref[0] + jnp.dot(mix[...], wout_ref[...], preferred_element_type=F32)


def ssd_mixer(proj, h, cw, cb, dtb, aneg, dsk, gnw, scw, wout_bf16):
    b, t, _ = h.shape
    tc = ROW_TILE
    const = lambda shape: pl.BlockSpec(shape, lambda i, j: (0,) * len(shape))
    return pl.pallas_call(
        _ssd_body,
        grid=(b, t // tc),
        in_specs=[pl.BlockSpec((1, tc, EVEN_PROJ), lambda i, j: (i, j, 0)),
                  pl.BlockSpec((1, tc, D_MODEL), lambda i, j: (i, j, 0)),
                  const((SSM_CONV, SSM_CONV_CH)), const((1, SSM_CONV_CH)),
                  const((1, LANES)), const((1, LANES)), const((1, SSM_INNER)), const((1, SSM_INNER)),
                  const((SCONV_WIDTH, SCONV_DIM)), const((EVEN_MIX, D_MODEL))],
        out_specs=pl.BlockSpec((1, tc, D_MODEL), lambda i, j: (i, j, 0)),
        out_shape=jax.ShapeDtypeStruct((b, t, D_MODEL), F32),
        scratch_shapes=[pltpu.VMEM((tc + SUBLANES, SSM_CONV_CH), F32),
                        pltpu.VMEM((tc, SSM_CONV_CH), F32),
                        pltpu.VMEM((tc, EVEN_MIX), BF16),
                        pltpu.VMEM((SSM_PAIRS, LANES, SSM_STATE), F32),
                        pltpu.VMEM((tc + SUBLANES, SCONV_DIM), F32)],
        compiler_params=_params("parallel", "arbitrary"),
        name="ssd_mixer",
    )(proj, h, cw, cb, dtb, aneg, dsk, gnw, scw, wout_bf16)


def even_layer(h, nw, w_in, conv_w, conv_b, dt_bias, a_log, d_skip, gnorm_w, sconv_w, w_out):
    b, t, d = h.shape
    o1 = SSM_INNER
    o2 = o1 + SSM_CONV_CH
    o3 = o2 + SSM_HEADS
    o4 = o3 + SCONV_DIM
    o5 = o4 + SCONV_DIM
    pad = jnp.zeros((d, LANES - SSM_HEADS), F32)
    w_re = jnp.concatenate([w_in[:, :o2], w_in[:, o3:], w_in[:, o2:o3], pad], axis=1).astype(BF16)
    proj = norm_matmul(h.reshape(b * t, d), nw, w_re, 512).reshape(b, t, EVEN_PROJ)
    lane_pad = lambda v: jnp.concatenate([v.astype(F32), jnp.zeros((LANES - SSM_HEADS,), F32)]).reshape(1, LANES)
    aneg = lane_pad(-jnp.exp(a_log.astype(F32)))
    dsk = jnp.repeat(d_skip.astype(F32), SSM_HEAD_DIM).reshape(1, SSM_INNER)
    del o4, o5
    return ssd_mixer(proj, h, conv_w, conv_b.reshape(1, -1), lane_pad(dt_bias), aneg, dsk,
                     gnorm_w.reshape(1, -1), sconv_w, w_out.astype(BF16))


def _qkv_body(x_ref, nw_ref, w_ref, cos_ref, sa_ref, sb_ref, qt_ref, k_ref, vt_ref):
    xn = _rms(x_ref[...], nw_ref[...]).astype(BF16)
    cos, sa, sb = cos_ref[...], sa_ref[...], sb_ref[...]

    def rope(a):
        return a * cos + pltpu.roll(a, ROPE_DIM // 2, 1) * sa + pltpu.roll(a, LANES - ROPE_DIM // 2, 1) * sb

    for c2 in range(0, ATT_HEADS, 2):
        proj = lambda part: jnp.dot(xn, w_ref[:, part * ATT_DIM + c2 * LANES:part * ATT_DIM + (c2 + 2) * LANES],
                                    preferred_element_type=F32)
        q2, k2, v2 = proj(0), proj(1), proj(2)
        for i in range(2):
            c = c2 + i
            lanes = slice(i * LANES, (i + 1) * LANES)
            qt_ref[0, c, 0] = (rope(q2[:, lanes]) * (ATT_HEAD_DIM ** -0.5 * LOG2E)).astype(BF16).T
            k_ref[:, c * LANES:(c + 1) * LANES] = rope(k2[:, lanes]).astype(BF16)
            vt_ref[0, c, 0, 0:LANES, :] = v2[:, lanes].astype(BF16).T
            row = lax.broadcasted_iota(jnp.int32, (VT_ROWS - LANES, ROW_TILE), 0)
            vt_ref[0, c, 0, LANES:VT_ROWS, :] = jnp.where(row == 0, 1.0, 0.0).astype(BF16)


def _rope_tables(t):
    half = ROPE_DIM // 2
    inv = 1.0 / (ROPE_THETA ** (jnp.arange(0, ROPE_DIM, 2, dtype=F32) / ROPE_DIM))
    ang = jnp.arange(t, dtype=F32)[:, None] * inv[None, :]
    cos, sin = jnp.cos(ang), jnp.sin(ang)
    rest = ATT_HEAD_DIM - ROPE_DIM
    one = jnp.ones((t, rest), F32)
    zero = jnp.zeros((t, rest), F32)
    zh = jnp.zeros((t, half), F32)
    cos64 = jnp.concatenate([cos, cos, one], axis=1)
    sa64 = jnp.concatenate([zh, sin, zero], axis=1)
    sb64 = jnp.concatenate([-sin, zh, zero], axis=1)
    tile2 = lambda v: jnp.concatenate([v, v], axis=1)
    return tile2(cos64), tile2(sa64), tile2(sb64)


def qkv_rope(x, nw, w_bf16, tables, b, t):
    n, d = x.shape
    assert ROW_TILE == ATT_TILE
    tiles = t // ROW_TILE
    tab = pl.BlockSpec((ROW_TILE, LANES), lambda i: (i % tiles, 0))
    tr = lambda rows: pl.BlockSpec((1, ATT_HEADS, 1, rows, ROW_TILE), lambda i: (i // tiles, 0, i % tiles, 0, 0))
    tr_shape = lambda rows: jax.ShapeDtypeStruct((b, ATT_HEADS, tiles, rows, ROW_TILE), BF16)
    return pl.pallas_call(
        _qkv_body,
        grid=(n // ROW_TILE,),
        in_specs=[pl.BlockSpec((ROW_TILE, d), lambda i: (i, 0)),
                  pl.BlockSpec((1, d), lambda i: (0, 0)),
                  pl.BlockSpec((d, 3 * ATT_DIM), lambda i: (0, 0)),
                  tab, tab, tab],
        out_specs=[tr(LANES), pl.BlockSpec((ROW_TILE, ATT_DIM), lambda i: (i, 0)), tr(VT_ROWS)],
        out_shape=[tr_shape(LANES), jax.ShapeDtypeStruct((n, ATT_DIM), BF16), tr_shape(VT_ROWS)],
        compiler_params=_params("parallel"),
        name="qkv_rope",
    )(x, nw.reshape(1, d), w_bf16, *tables)


def _attn_body(qt_ref, k_ref, vt_ref, lam_ref, sw_ref, o_ref, m_scr, acc_scr, *, lambda_init):
    tq = ATT_TILE
    qi = pl.program_id(2)
    feat = lax.broadcasted_iota(jnp.int32, (LANES, tq), 0)
    rk = lax.broadcasted_iota(jnp.int32, (tq, 2 * tq), 0)
    rq = lax.broadcasted_iota(jnp.int32, (tq, 2 * tq), 1)
    rq = jnp.where(rq >= tq, rq - tq, rq)
    visible = (rk // CHUNK) <= (rq // CHUNK)

    qs = []
    for g in range(ATT_GROUP):
        qt = qt_ref[0, g, 0]
        zero = jnp.zeros_like(qt)
        qs.append(jnp.concatenate([jnp.where(feat < ATT_HEAD_DIM, qt, zero),
                                   jnp.where(feat < ATT_HEAD_DIM, zero, qt)], axis=1))

    def scores(g, j, ntile):
        off = pl.multiple_of(j * tq, tq)
        return jnp.dot(k_ref[0, pl.ds(off, ntile * tq), g * LANES:(g + 1) * LANES], qs[g],
                       preferred_element_type=F32)

    def pv(g, j, ntile, p):
        out = jnp.dot(vt_ref[0, g, j], p[0:tq], preferred_element_type=F32)
        for i in range(1, ntile):
            out = out + jnp.dot(vt_ref[0, g, j + i], p[i * tq:(i + 1) * tq], preferred_element_type=F32)
        return out

    for g in range(ATT_GROUP):
        s = jnp.where(visible, scores(g, qi, 1), -jnp.inf)
        m0 = jnp.max(s, axis=0, keepdims=True)
        m_scr[g] = m0
        acc_scr[g] = pv(g, qi, 1, jnp.exp2(s - m0).astype(BF16))

    def update(j, ntile):
        for g in range(ATT_GROUP):
            s = scores(g, j, ntile)
            m_old = m_scr[g]
            m_new = jnp.maximum(m_old, jnp.max(s, axis=0, keepdims=True))
            m_scr[g] = m_new
            acc_scr[g] = jnp.exp2(m_old - m_new) * acc_scr[g] + pv(g, j, ntile, jnp.exp2(s - m_new).astype(BF16))

    @pl.when(qi % 2 == 1)
    def _():
        update(0, 1)

    @pl.when((qi // 2) % 2 == 1)
    def _():
        update(qi % 2, 2)

    def body(t, carry):
        update(qi % 4 + 4 * t, 4)
        return carry

    lax.fori_loop(0, qi // 4, body, 0)

    lp = lam_ref[...]
    lam = (jnp.exp(jnp.sum(lp[0:1] * lp[1:2], axis=-1, keepdims=True))
           - jnp.exp(jnp.sum(lp[2:3] * lp[3:4], axis=-1, keepdims=True)) + lambda_init)
    for g in range(ATT_GROUP):
        acc = acc_scr[g]
        on = acc[0:LANES, :] / acc[LANES:LANES + 1, :]
        o = (on[:, 0:tq] - lam * on[:, tq:2 * tq]).T
        o_ref[0, :, g * LANES:(g + 1) * LANES] = (_rms(o, sw_ref[...]) * (1.0 - lambda_init)).astype(o_ref.dtype)


def diff_attention(qt, k, vt, lam_p, subln_w, lambda_init):
    b, t, _ = k.shape
    tq = ATT_TILE
    tiles = t // tq
    g = ATT_GROUP
    return pl.pallas_call(
        functools.partial(_attn_body, lambda_init=lambda_init),
        grid=(b, ATT_HEADS // g, tiles),
        in_specs=[pl.BlockSpec((1, g, 1, LANES, tq), lambda i, h, j: (i, h, j, 0, 0)),
                  pl.BlockSpec((1, t, g * LANES), lambda i, h, j: (i, 0, h)),
                  pl.BlockSpec((1, g, tiles, VT_ROWS, tq), lambda i, h, j: (i, h, 0, 0, 0)),
                  pl.BlockSpec((4, ATT_HEAD_DIM), lambda i, h, j: (0, 0)),
                  pl.BlockSpec((1, LANES), lambda i, h, j: (0, 0))],
        out_specs=pl.BlockSpec((1, tq, g * LANES), lambda i, h, j: (i, j, h)),
        out_shape=jax.ShapeDtypeStruct((b, t, ATT_DIM), BF16),
        scratch_shapes=[pltpu.VMEM((g, 1, 2 * tq), F32), pltpu.VMEM((g, VT_ROWS, 2 * tq), F32)],
        compiler_params=_params("parallel", "parallel", "arbitrary"),
        name="diff_attention",
    )(qt, k, vt, lam_p, subln_w.reshape(1, LANES))


def _matmul_res_body(a_ref, w_ref, h_ref, o_ref):
    o_ref[...] = h_ref[...] + jnp.dot(a_ref[...], w_ref[...], preferred_element_type=F32)


def matmul_residual(a_bf16, w_bf16, h):
    n, kdim = a_bf16.shape
    d = w_bf16.shape[1]
    return pl.pallas_call(
        _matmul_res_body,
        grid=(n // ROW_TILE,),
        in_specs=[pl.BlockSpec((ROW_TILE, kdim), lambda i: (i, 0)),
                  pl.BlockSpec((kdim, d), lambda i: (0, 0)),
                  pl.BlockSpec((ROW_TILE, d), lambda i: (i, 0))],
        out_specs=pl.BlockSpec((ROW_TILE, d), lambda i: (i, 0)),
        out_shape=jax.ShapeDtypeStruct((n, d), F32),
        compiler_params=_params("parallel"),
        name="matmul_residual",
    )(a_bf16, w_bf16, h)


def odd_layer(h, nw, w_qkv, lam_p, subln_w, w_out, lambda_init, tables):
    b, t, d = h.shape
    qt, k, vt = qkv_rope(h.reshape(b * t, d), nw, w_qkv.astype(BF16), tables, b, t)
    o = diff_attention(qt, k.reshape(b, t, ATT_DIM), vt, lam_p.astype(F32), subln_w.astype(F32), lambda_init)
    return matmul_residual(o.reshape(b * t, ATT_DIM), w_out.astype(BF16), h.reshape(b * t, d)).reshape(b, t, d)


META_E, META_R, META_G = 0, 2, 4


def _router_body(h_ref, nw_ref, wr_ref, br_ref, before_ref, meta_ref, cnt_ref, run_cnt):
    tm = ROUTER_TILE

    @pl.when(pl.program_id(0) == 0)
    def _():
        run_cnt[...] = jnp.zeros_like(run_cnt)

    xn = _rms(h_ref[...], nw_ref[...])
    x_hi = xn.astype(BF16)
    x_lo = (xn - x_hi.astype(F32)).astype(BF16)
    logits = (jnp.dot(x_hi, wr_ref[0], preferred_element_type=F32)
              + (jnp.dot(x_lo, wr_ref[0], preferred_element_type=F32)
                 + jnp.dot(x_hi, wr_ref[1], preferred_element_type=F32))) + br_ref[...]
    lane = lax.broadcasted_iota(jnp.int32, (tm, LANES), 1)
    big = jnp.int32(LANES)
    neg = -jnp.inf

    is_g = lane < MOE_GROUPS
    gl = jnp.where(is_g, logits, neg)
    gexp = jnp.exp(gl - jnp.max(gl, axis=-1, keepdims=True))
    gprob = gexp / jnp.sum(gexp, axis=-1, keepdims=True)
    g_gate = jnp.max(gprob, axis=-1, keepdims=True)
    g_idx = jnp.min(jnp.where(is_g & (gprob == g_gate), lane, big), axis=-1, keepdims=True)

    lo = ROUTER_LANE0 + g_idx * EXPERTS_PER_GROUP
    sel = (lane >= lo) & (lane < lo + EXPERTS_PER_GROUP)
    el = jnp.where(sel, logits, neg)
    eexp = jnp.exp(el - jnp.max(el, axis=-1, keepdims=True))
    eprob = jnp.where(sel, eexp / jnp.sum(eexp, axis=-1, keepdims=True), -1.0)
    p1 = jnp.max(eprob, axis=-1, keepdims=True)
    i1 = jnp.min(jnp.where(eprob == p1, lane, big), axis=-1, keepdims=True)
    rest = jnp.where(lane == i1, -1.0, eprob)
    p2 = jnp.max(rest, axis=-1, keepdims=True)
    i2 = jnp.min(jnp.where(rest == p2, lane, big), axis=-1, keepdims=True)
    g1 = g_gate * p1 / (p1 + p2)
    g2 = g_gate * p2 / (p1 + p2)

    hit1 = lane == i1
    hit2 = lane == i2
    assign = jnp.where(hit1 | hit2, 1.0, 0.0)
    prior = jnp.dot(before_ref[...], assign.astype(BF16), preferred_element_type=F32) + run_cnt[...]
    r1 = jnp.sum(jnp.where(hit1, prior, 0.0), axis=-1, keepdims=True)
    r2 = jnp.sum(jnp.where(hit2, prior, 0.0), axis=-1, keepdims=True)
    run_cnt[...] = run_cnt[...] + jnp.sum(assign, axis=0, keepdims=True)
    cnt_ref[...] = jnp.broadcast_to(run_cnt[...], cnt_ref.shape)

    e1 = (i1 - ROUTER_LANE0).astype(F32)
    e2 = (i2 - ROUTER_LANE0).astype(F32)
    rec = jnp.zeros((tm, LANES), F32)
    for pos, val in ((META_E, e1), (META_E + 1, e2), (META_R, r1), (META_R + 1, r2), (META_G, g1), (META_G + 1, g2)):
        rec = jnp.where(lane == pos, val, rec)
    meta_ref[...] = rec


def moe_router(h, nw, w_router, b_router):
    n, d = h.shape
    tm = ROUTER_TILE
    before = jnp.tril(jnp.ones((tm, tm), BF16), -1)
    return pl.pallas_call(
        _router_body,
        grid=(n // tm,),
        in_specs=[pl.BlockSpec((tm, d), lambda i: (i, 0)),
                  pl.BlockSpec((1, d), lambda i: (0, 0)),
                  pl.BlockSpec((2, d, LANES), lambda i: (0, 0, 0)),
                  pl.BlockSpec((1, LANES), lambda i: (0, 0)),
                  pl.BlockSpec((tm, tm), lambda i: (0, 0))],
        out_specs=[pl.BlockSpec((tm, LANES), lambda i: (i, 0)),
                   pl.BlockSpec((SUBLANES, LANES), lambda i: (0, 0))],
        out_shape=[jax.ShapeDtypeStruct((n, LANES), F32), jax.ShapeDtypeStruct((SUBLANES, LANES), F32)],
        scratch_shapes=[pltpu.VMEM((1, LANES), F32)],
        compiler_params=_params("arbitrary"),
        name="moe_router",
    )(h, nw.reshape(1, d), w_router, b_router, before)


def _row_copy(src, src_row, dst, dst_row, sem):
    return pltpu.make_async_copy(src.at[pl.ds(src_row, 1), :], dst.at[pl.ds(dst_row, 1), :], sem)


def _pack_rows(x):
    bits = lax.bitcast_convert_type(x.astype(BF16).astype(F32), jnp.uint32)
    return (bits[:, :PACKED] >> 16) | bits[:, PACKED:]


def _unpack_rows(u):
    lo = lax.bitcast_convert_type(u << 16, F32)
    hi = lax.bitcast_convert_type(u & jnp.uint32(0xFFFF0000), F32)
    return lo, hi


def _dispatch_body(pends_ref, padded_ref, dest_ref, h_ref, nw_ref, xrows_ref, xp_scr, zbuf, sem, zsem):
    tm = MOE_TILE

    @pl.when(pl.program_id(0) == 0)
    def _():
        zbuf[...] = jnp.zeros_like(zbuf)

        def tail(e):
            return xrows_ref.at[pl.ds(pl.multiple_of(pends_ref[e] - MOE_BLOCK, MOE_BLOCK), MOE_BLOCK), :]

        def fill(e, carry):
            @pl.when(padded_ref[e] > 0)
            def _():
                pltpu.make_async_copy(zbuf, tail(e), zsem).start()
            return carry

        def drain(e, carry):
            @pl.when(padded_ref[e] > 0)
            def _():
                pltpu.make_async_copy(zbuf, tail(e), zsem).wait()
            return carry

        def block(bk):
            return xrows_ref.at[pl.ds(pl.multiple_of(bk * MOE_BLOCK, MOE_BLOCK), MOE_BLOCK), :]

        def fill_block(bk, carry):
            pltpu.make_async_copy(zbuf, block(bk), zsem).start()
            return carry

        def drain_block(bk, carry):
            pltpu.make_async_copy(zbuf, block(bk), zsem).wait()
            return carry

        first_unused = pends_ref[N_EXPERTS - 1] // MOE_BLOCK
        n_blk = xrows_ref.shape[0] // MOE_BLOCK
        lax.fori_loop(0, N_EXPERTS, fill, 0)
        lax.fori_loop(first_unused, n_blk, fill_block, 0)
        lax.fori_loop(0, N_EXPERTS, drain, 0)
        lax.fori_loop(first_unused, n_blk, drain_block, 0)

    xp_scr[...] = _pack_rows(_rms(h_ref[...], nw_ref[...]))

    def issue(i, carry):
        _row_copy(xp_scr, i, xrows_ref, dest_ref[0, 0, 2 * i], sem).start()
        _row_copy(xp_scr, i, xrows_ref, dest_ref[0, 0, 2 * i + 1], sem).start()
        return carry

    lax.fori_loop(0, tm, issue, 0, unroll=DMA_UNROLL)
    for _ in range(2):
        pltpu.make_async_copy(xp_scr, xrows_ref.at[pl.ds(0, tm), :], sem).wait()


def moe_dispatch(h, nw, dest, pends, padded, n_rows):
    n, d = h.shape
    tm = MOE_TILE
    return pl.pallas_call(
        _dispatch_body,
        grid_spec=pltpu.PrefetchScalarGridSpec(
            num_scalar_prefetch=2,
            grid=(n // tm,),
            in_specs=[pl.BlockSpec((1, 1, 2 * tm), lambda i, pe, pa: (i, 0, 0), memory_space=pltpu.SMEM),
                      pl.BlockSpec((tm, d), lambda i, pe, pa: (i, 0)),
                      pl.BlockSpec((1, d), lambda i, pe, pa: (0, 0))],
            out_specs=pl.BlockSpec(memory_space=pl.ANY),
            scratch_shapes=[pltpu.VMEM((tm, PACKED), jnp.uint32), pltpu.VMEM((MOE_BLOCK, PACKED), jnp.uint32),
                            pltpu.SemaphoreType.DMA, pltpu.SemaphoreType.DMA]),
        out_shape=jax.ShapeDtypeStruct((n_rows, PACKED), jnp.uint32),
        compiler_params=_params("arbitrary"),
        name="moe_dispatch",
    )(pends, padded, dest.reshape(n // tm, 1, 2 * tm), h, nw.reshape(1, d))


def _ffn_body(pstart_ref, nblk_ref, w1_ref, w3_ref, w2_ref, xrows_ref, yrows_ref,
              w1b, w3b, w2b, xbuf, ybuf, xsem, ysem):
    e = pl.program_id(0)
    nb = nblk_ref[e]
    g0 = pstart_ref[e] // MOE_BLOCK
    n_used = (pstart_ref[N_EXPERTS - 1] // MOE_BLOCK) + nblk_ref[N_EXPERTS - 1]
    n_blk = yrows_ref.shape[0] // MOE_BLOCK

    def rows(g):
        return pl.ds(pl.multiple_of(g * MOE_BLOCK, MOE_BLOCK), MOE_BLOCK)

    def x_copy(g):
        slot = g % (FFN_AHEAD + 1)
        return pltpu.make_async_copy(xrows_ref.at[rows(g), :], xbuf.at[slot], xsem.at[slot])

    def y_copy(g):
        slot = g % 2
        return pltpu.make_async_copy(ybuf.at[slot], yrows_ref.at[rows(g), :], ysem.at[slot])

    @pl.when(e == 0)
    def _():
        for g in range(FFN_AHEAD):
            @pl.when(g < n_used)
            def _():
                x_copy(g).start()

    @pl.when(nb > 0)
    def _():
        w1b[...] = w1_ref[0, 0].astype(BF16)
        w3b[...] = w3_ref[0, 0].astype(BF16)
        w2b[...] = w2_ref[0, 0].astype(BF16)

        def block(i, carry):
            g = g0 + i
            x_copy(g).wait()

            @pl.when(g + FFN_AHEAD < n_used)
            def _():
                x_copy(g + FFN_AHEAD).start()

            @pl.when(g >= 2)
            def _():
                y_copy(g - 2).wait()

            lo, hi = _unpack_rows(xbuf[g % (FFN_AHEAD + 1)])
            x = jnp.concatenate([lo.astype(BF16), hi.astype(BF16)], axis=1)
            u = jnp.dot(x, w1b[...], preferred_element_type=F32)
            gate = jnp.dot(x, w3b[...], preferred_element_type=F32)
            ybuf[g % 2] = _pack_rows(jnp.dot((_silu(u) * gate).astype(BF16), w2b[...], preferred_element_type=F32))
            y_copy(g).start()
            return carry

        lax.fori_loop(0, nb, block, 0)

    @pl.when(e == N_EXPERTS - 1)
    def _():
        @pl.when(n_used >= 2)
        def _():
            y_copy(n_used - 2).wait()

        y_copy(n_used - 1).wait()

        ybuf[0] = jnp.zeros((MOE_BLOCK, PACKED), jnp.uint32)

        def tail(bk):
            return pltpu.make_async_copy(ybuf.at[0], yrows_ref.at[rows(bk), :], ysem.at[0])

        def fill(bk, carry):
            tail(bk).start()
            return carry

        def drain(bk, carry):
            tail(bk).wait()
            return carry

        lax.fori_loop(n_used, n_blk, fill, 0)
        lax.fori_loop(n_used, n_blk, drain, 0)


def moe_ffn(x_rows, pstarts, nblk, w1, w3, w2, layer):
    n_rows = x_rows.shape[0]
    d, de = w1.shape[-2:]
    wsel = lambda e, ps, nb: (layer, e, 0, 0)
    return pl.pallas_call(
        _ffn_body,
        grid_spec=pltpu.PrefetchScalarGridSpec(
            num_scalar_prefetch=2,
            grid=(N_EXPERTS,),
            in_specs=[pl.BlockSpec((1, 1, d, de), wsel),
                      pl.BlockSpec((1, 1, d, de), wsel),
                      pl.BlockSpec((1, 1, de, d), wsel),
                      pl.BlockSpec(memory_space=pl.ANY)],
            out_specs=pl.BlockSpec(memory_space=pl.ANY),
            scratch_shapes=[pltpu.VMEM((d, de), BF16), pltpu.VMEM((d, de), BF16), pltpu.VMEM((de, d), BF16),
                            pltpu.VMEM((FFN_AHEAD + 1, MOE_BLOCK, PACKED), jnp.uint32),
                            pltpu.VMEM((2, MOE_BLOCK, PACKED), jnp.uint32),
                            pltpu.SemaphoreType.DMA((FFN_AHEAD + 1,)), pltpu.SemaphoreType.DMA((2,))]),
        out_shape=jax.ShapeDtypeStruct((n_rows, PACKED), jnp.uint32),
        compiler_params=_params("arbitrary"),
        name="moe_ffn",
    )(pstarts, nblk, w1, w3, w2, x_rows)


def _combine_body(dest_ref, meta_ref, h_ref, yrows_ref, fw_ref, o_ref, ybuf, sem, *, final_norm):
    tm = MOE_TILE

    def issue(i, carry):
        _row_copy(yrows_ref, dest_ref[0, 0, 2 * i], ybuf.at[0], i, sem).start()
        _row_copy(yrows_ref, dest_ref[0, 0, 2 * i + 1], ybuf.at[1], i, sem).start()
        return carry

    lax.fori_loop(0, tm, issue, 0, unroll=DMA_UNROLL)
    for half in range(2):
        pltpu.make_async_copy(yrows_ref.at[pl.ds(0, tm), :], ybuf.at[half], sem).wait()

    meta = meta_ref[...]
    g1 = meta[:, META_G:META_G + 1]
    g2 = meta[:, META_G + 1:META_G + 2]
    lo1, hi1 = _unpack_rows(ybuf[0])
    lo2, hi2 = _unpack_rows(ybuf[1])
    out = h_ref[...] + jnp.concatenate([g1 * lo1 + g2 * lo2, g1 * hi1 + g2 * hi2], axis=1)
    if final_norm:
        out = _rms(out, fw_ref[...])
    o_ref[...] = out


def moe_combine(h, meta, dest, y_rows, final_w):
    n, d = h.shape
    tm = MOE_TILE
    final_norm = final_w is not None
    fw = (final_w if final_norm else jnp.ones((d,), F32)).reshape(1, d)
    return pl.pallas_call(
        functools.partial(_combine_body, final_norm=final_norm),
        grid=(n // tm,),
        in_specs=[pl.BlockSpec((1, 1, 2 * tm), lambda i: (i, 0, 0), memory_space=pltpu.SMEM),
                  pl.BlockSpec((tm, LANES), lambda i: (i, 0)),
                  pl.BlockSpec((tm, d), lambda i: (i, 0)),
                  pl.BlockSpec(memory_space=pl.ANY),
                  pl.BlockSpec((1, d), lambda i: (0, 0))],
        out_specs=pl.BlockSpec((tm, d), lambda i: (i, 0)),
        out_shape=jax.ShapeDtypeStruct((n, d), F32),
        scratch_shapes=[pltpu.VMEM((2, tm, PACKED), jnp.uint32), pltpu.SemaphoreType.DMA],
        compiler_params=_params("arbitrary"),
        name="moe_combine",
    )(dest.reshape(n // tm, 1, 2 * tm), meta, h, y_rows, fw)


def hier_moe_layer(h, nw, w_group, b_group, w_expert, b_expert, w1, w3, w2, layer, final_w):
    n, d = h.shape
    pad_cols = LANES - MOE_GROUPS - N_EXPERTS
    w_router = jnp.concatenate([w_group, w_expert, jnp.zeros((d, pad_cols), F32)], axis=1)
    w_hi = w_router.astype(BF16)
    w_router = jnp.stack([w_hi, (w_router - w_hi.astype(F32)).astype(BF16)])
    b_router = jnp.concatenate([b_group, b_expert, jnp.zeros((pad_cols,), F32)]).reshape(1, LANES)
    meta, cnt = moe_router(h, nw, w_router, b_router)

    counts = cnt[0, ROUTER_LANE0:ROUTER_LANE0 + N_EXPERTS].astype(jnp.int32)
    padded = (counts + MOE_BLOCK - 1) // MOE_BLOCK * MOE_BLOCK
    pends = jnp.cumsum(padded)
    pstarts = pends - padded
    expert = meta[:, META_E:META_E + 2].astype(jnp.int32)
    rank = meta[:, META_R:META_R + 2].astype(jnp.int32)
    hit = expert[:, :, None] == jnp.arange(N_EXPERTS, dtype=jnp.int32)
    dest = (jnp.sum(jnp.where(hit, pstarts, 0), axis=-1) + rank).reshape(n * 2)
    n_rows = n * 2 + N_EXPERTS * MOE_BLOCK

    x_rows = moe_dispatch(h, nw, dest, pends.astype(jnp.int32), padded.astype(jnp.int32), n_rows)
    y_rows = moe_ffn(x_rows, pstarts.astype(jnp.int32), (padded // MOE_BLOCK).astype(jnp.int32), w1, w3, w2, layer)
    return moe_combine(h, meta, dest, y_rows, final_w)


def _lambda_init(layer):
    return 0.8 - 0.6 * math.exp(-0.3 * layer)


def kernel(x, norm_mix_w, norm_ffn_w, final_norm_w, ev_w_in, ev_conv_w, ev_conv_b, ev_dt_bias, ev_a_log,
           ev_d_skip, ev_gnorm_w, ev_sconv_w, ev_w_out, od_w_qkv, od_lambda, od_subln_w, od_w_out,
           moe_w_group, moe_b_group, moe_w_expert, moe_b_expert, moe_w1, moe_w3, moe_w2):
    b, t, d = x.shape
    depth = norm_mix_w.shape[0]
    tables = _rope_tables(t)
    h = x
    for layer in range(depth):
        i = layer // 2
        if layer % 2 == 0:
            h = even_layer(h, norm_mix_w[layer], ev_w_in[i], ev_conv_w[i], ev_conv_b[i], ev_dt_bias[i],
                           ev_a_log[i], ev_d_skip[i], ev_gnorm_w[i], ev_sconv_w[i], ev_w_out[i])
        else:
            h = odd_layer(h, norm_mix_w[layer], od_w_qkv[i], od_lambda[i], od_subln_w[i], od_w_out[i],
                          _lambda_init(layer), tables)
        final_w = final_norm_w if layer == depth - 1 else None
        h = hier_moe_layer(h.reshape(b * t, d), norm_ffn_w[layer], moe_w_group[layer], moe_b_group[layer],
                           moe_w_expert[layer], moe_b_expert[layer], moe_w1, moe_w3, moe_w2, layer,
                           final_w).reshape(b, t, d)
    return h
```

```python
import functools
import math

import jax
import jax.numpy as jnp
from jax import lax
from jax.experimental import pallas as pl
from jax.experimental.pallas import tpu as pltpu

F32 = jnp.float32
BF16 = jnp.bfloat16

D_MODEL = 1024
CHUNK = 64
SSM_INNER = 1536
SSM_HEAD_DIM = 64
SSM_HEADS = 24
SSM_GROUPS = 4
SSM_STATE = 128
SSM_CONV = 4
SSM_CONV_CH = SSM_INNER + 2 * SSM_GROUPS * SSM_STATE
SSM_PAIRS = SSM_HEADS // 2
PAIRS_PER_GROUP = SSM_PAIRS // SSM_GROUPS
GROUP_CH = SSM_INNER // SSM_GROUPS
SCONV_DIM = 512
SCONV_WIDTH = 3
EVEN_MIX = SSM_INNER + SCONV_DIM
ATT_HEADS = 8
ATT_HEAD_DIM = 64
ATT_DIM = ATT_HEADS * 2 * ATT_HEAD_DIM
ROPE_DIM = ATT_HEAD_DIM // 4
ROPE_THETA = 500000.0
MOE_GROUPS = 8
EXPERTS_PER_GROUP = 8
N_EXPERTS = 64
D_EXPERT = 512
EPS = 1e-6

LANES = 128
SUBLANES = 8
VMEM_LIMIT = 56 * 1024 * 1024

OFF_Z = 0
OFF_XBC = SSM_INNER
OFF_HB = OFF_XBC + SSM_CONV_CH
OFF_BG = OFF_HB + SCONV_DIM
OFF_CG = OFF_BG + SCONV_DIM
OFF_DT = OFF_CG + SCONV_DIM
EVEN_PROJ = OFF_DT + LANES

ROW_TILE = 256
SSD_CHUNK = 256
ATT_TILE = 256
ATT_GROUP = 8
VT_ROWS = LANES + 16
LOG2E = 1.4426950408889634
ROUTER_TILE = 512
MOE_TILE = 1024
MOE_BLOCK = 256
FFN_AHEAD = 3
PACKED = D_MODEL // 2
DMA_UNROLL = 8
ROUTER_LANE0 = MOE_GROUPS


def _params(*sem):
    return pltpu.CompilerParams(dimension_semantics=sem, vmem_limit_bytes=VMEM_LIMIT)


def _rms(x, w):
    return x * lax.rsqrt(jnp.mean(x * x, axis=-1, keepdims=True) + EPS) * w


def _silu(x):
    return x * (1.0 / (1.0 + jnp.exp(-x)))


def _norm_matmul_body(x_ref, nw_ref, w_ref, o_ref, *, col_chunk):
    xn = _rms(x_ref[...], nw_ref[...]).astype(BF16)
    for c0 in range(0, o_ref.shape[-1], col_chunk):
        o_ref[:, c0:c0 + col_chunk] = jnp.dot(
            xn, w_ref[:, c0:c0 + col_chunk], preferred_element_type=F32).astype(o_ref.dtype)


def norm_matmul(x, nw, w_bf16, col_chunk):
    n, d = x.shape
    f = w_bf16.shape[1]
    return pl.pallas_call(
        functools.partial(_norm_matmul_body, col_chunk=col_chunk),
        grid=(n // ROW_TILE,),
        in_specs=[pl.BlockSpec((ROW_TILE, d), lambda i: (i, 0)),
                  pl.BlockSpec((1, d), lambda i: (0, 0)),
                  pl.BlockSpec((d, f), lambda i: (0, 0))],
        out_specs=pl.BlockSpec((ROW_TILE, f), lambda i: (i, 0)),
        out_shape=jax.ShapeDtypeStruct((n, f), F32),
        compiler_params=_params("parallel"),
        name="norm_matmul",
    )(x, nw.reshape(1, d), w_bf16)


def _softplus(x):
    return jnp.maximum(x, 0.0) + jnp.log(1.0 + jnp.exp(-jnp.abs(x)))


def _ssd_body(proj_ref, h_ref, cw_ref, cb_ref, dtb_ref, aneg_ref, dsk_ref, gnw_ref, scw_ref, wout_ref,
              o_ref, cbuf, xact, mix, state, sbuf):
    tc = ROW_TILE
    halo = SUBLANES

    @pl.when(pl.program_id(1) == 0)
    def _():
        state[...] = jnp.zeros_like(state)
        cbuf[0:halo, :] = jnp.zeros((halo, SSM_CONV_CH), F32)
        sbuf[0:halo, :] = jnp.zeros((halo, SCONV_DIM), F32)

    cbuf[halo:halo + tc, :] = proj_ref[0, :, OFF_XBC:OFF_XBC + SSM_CONV_CH]
    for c0 in range(0, SSM_CONV_CH, 512):
        xin = cbuf[:, c0:c0 + 512]
        acc = cb_ref[:, c0:c0 + 512] + cw_ref[SSM_CONV - 1:SSM_CONV, c0:c0 + 512] * xin
        for back in range(1, SSM_CONV):
            j = SSM_CONV - 1 - back
            acc = acc + cw_ref[j:j + 1, c0:c0 + 512] * pltpu.roll(xin, back, 0)
        xact[:, c0:c0 + 512] = _silu(acc[halo:halo + tc])
    cbuf[0:halo, :] = cbuf[tc:tc + halo, :]

    sbuf[halo:halo + tc, :] = (proj_ref[0, :, OFF_CG:OFF_CG + SCONV_DIM]
                               * proj_ref[0, :, OFF_HB:OFF_HB + SCONV_DIM])
    sin = sbuf[...]
    conv = scw_ref[SCONV_WIDTH - 1:SCONV_WIDTH, :] * sin
    for back in range(1, SCONV_WIDTH):
        j = SCONV_WIDTH - 1 - back
        conv = conv + scw_ref[j:j + 1, :] * pltpu.roll(sin, back, 0)
    mix[:, SSM_INNER:EVEN_MIX] = (proj_ref[0, :, OFF_BG:OFF_BG + SCONV_DIM] * conv[halo:halo + tc]).astype(BF16)
    sbuf[0:halo, :] = sbuf[tc:tc + halo, :]

    L = SSD_CHUNK
    lane_p = lax.broadcasted_iota(jnp.int32, (L, LANES), 1)
    p_lo = lane_p < SSM_HEAD_DIM
    sub_s = lax.broadcasted_iota(jnp.int32, (L, 2 * L), 0)
    lane_s = lax.broadcasted_iota(jnp.int32, (L, 2 * L), 1)
    s_lo = lane_s < L
    tril2 = sub_s >= jnp.where(s_lo, lane_s, lane_s - L)
    csum_l = lax.broadcasted_iota(jnp.int32, (L, L), 0)
    csum_s = lax.broadcasted_iota(jnp.int32, (L, L), 1)
    tril_incl = (csum_s <= csum_l).astype(F32)
    sub2 = lax.broadcasted_iota(jnp.int32, (LANES, LANES), 0)
    rhs_top = lax.broadcasted_iota(jnp.int32, (2 * L, LANES), 0) < L
    rhs_lo = lax.broadcasted_iota(jnp.int32, (2 * L, LANES), 1) < SSM_HEAD_DIM

    def chunk_body(c, carry):
        r0 = pl.multiple_of(c * L, L)
        rows = pl.ds(r0, L)
        dt = _softplus(proj_ref[0, rows, OFF_DT:OFF_DT + LANES] + dtb_ref[...])
        a = dt * aneg_ref[...]
        acum = jnp.dot(tril_incl, a, preferred_element_type=F32, precision=lax.Precision.HIGHEST)
        acum_t = acum.T
        for g in range(SSM_GROUPS):
            bg = xact[rows, SSM_INNER + g * SSM_STATE:SSM_INNER + (g + 1) * SSM_STATE].astype(BF16)
            c_off = SSM_INNER + SSM_GROUPS * SSM_STATE
            cg = xact[rows, c_off + g * SSM_STATE:c_off + (g + 1) * SSM_STATE].astype(BF16)
            cb = lax.dot_general(cg, bg, (((1,), (1,)), ((), ())), preferred_element_type=F32)
            cb2 = jnp.concatenate([cb, cb], axis=1)
            ys = []
            for jj in range(PAIRS_PER_GROUP):
                j = g * PAIRS_PER_GROUP + jj
                h0, h1 = 2 * j, 2 * j + 1
                xp = xact[rows, j * LANES:(j + 1) * LANES]
                dcol0, dcol1 = dt[:, h0:h0 + 1], dt[:, h1:h1 + 1]
                acol0, acol1 = acum[:, h0:h0 + 1], acum[:, h1:h1 + 1]
                dt_p = jnp.where(p_lo, dcol0, dcol1)
                col_p = jnp.where(p_lo, acol0, acol1)
                col_s = jnp.where(s_lo, acol0, acol1)
                row_s = jnp.concatenate([acum_t[h0:h0 + 1, :], acum_t[h1:h1 + 1, :]], axis=1)
                last_p = col_p[L - 1:L, :]
                xdt = xp * dt_p
                m2 = cb2 * jnp.exp2(jnp.where(tril2, col_s - row_s, -jnp.inf))
                xdt2 = jnp.concatenate([xdt, xdt], axis=0)
                rhs = jnp.where(rhs_top == rhs_lo, xdt2, 0.0).astype(BF16)
                y2 = jnp.dot(m2.astype(BF16), rhs, preferred_element_type=F32)
                s_old = state[j]
                y_in = lax.dot_general(cg, s_old.astype(BF16), (((1,), (1,)), ((), ())),
                                       preferred_element_type=F32)
                y2 = y2 + y_in * jnp.exp2(col_p) + dsk_ref[:, j * LANES:(j + 1) * LANES] * xp
                xd = (xdt * jnp.exp2(last_p - col_p)).astype(BF16)
                upd = lax.dot_general(xd, bg, (((0,), (0,)), ((), ())), preferred_element_type=F32)
                e0 = jnp.exp2(acum_t[h0:h0 + 1, L - 1:L])
                e1 = jnp.exp2(acum_t[h1:h1 + 1, L - 1:L])
                state[j] = s_old * jnp.where(sub2 < SSM_HEAD_DIM, e0, e1) + upd
                ys.append(y2)
            yg = jnp.concatenate(ys, axis=1)
            yg = yg * _silu(proj_ref[0, rows, OFF_Z + g * GROUP_CH:OFF_Z + (g + 1) * GROUP_CH])
            mix[rows, g * GROUP_CH:(g + 1) * GROUP_CH] = _rms(
                yg, gnw_ref[:, g * GROUP_CH:(g + 1) * GROUP_CH]).astype(BF16)
        return carry

    lax.fori_loop(0, tc // SSD_CHUNK, chunk_body, 0)
    o_ref[0] = h_ref[0] + jnp.dot(mix[...], wout_ref[...], preferred_element_type=F32)


def ssd_mixer(proj, h, cw, cb, dtb, aneg, dsk, gnw, scw, wout_bf16):
    b, t, _ = h.shape
    tc = ROW_TILE
    const = lambda shape: pl.BlockSpec(shape, lambda i, j: (0,) * len(shape))
    return pl.pallas_call(
        _ssd_body,
        grid=(b, t // tc),
        in_specs=[pl.BlockSpec((1, tc, EVEN_PROJ), lambda i, j: (i, j, 0)),
                  pl.BlockSpec((1, tc, D_MODEL), lambda i, j: (i, j, 0)),
                  const((SSM_CONV, SSM_CONV_CH)), const((1, SSM_CONV_CH)),
                  const((1, LANES)), const((1, LANES)), const((1, SSM_INNER)), const((1, SSM_INNER)),
                  const((SCONV_WIDTH, SCONV_DIM)), const((EVEN_MIX, D_MODEL))],
        out_specs=pl.BlockSpec((1, tc, D_MODEL), lambda i, j: (i, j, 0)),
        out_shape=jax.ShapeDtypeStruct((b, t, D_MODEL), F32),
        scratch_shapes=[pltpu.VMEM((tc + SUBLANES, SSM_CONV_CH), F32),
                        pltpu.VMEM((tc, SSM_CONV_CH), F32),
                        pltpu.VMEM((tc, EVEN_MIX), BF16),
                        pltpu.VMEM((SSM_PAIRS, LANES, SSM_STATE), F32),
                        pltpu.VMEM((tc + SUBLANES, SCONV_DIM), F32)],
        compiler_params=_params("parallel", "arbitrary"),
        name="ssd_mixer",
    )(proj, h, cw, cb, dtb, aneg, dsk, gnw, scw, wout_bf16)


def even_layer(h, nw, w_in, conv_w, conv_b, dt_bias, a_log, d_skip, gnorm_w, sconv_w, w_out):
    b, t, d = h.shape
    o1 = SSM_INNER
    o2 = o1 + SSM_CONV_CH
    o3 = o2 + SSM_HEADS
    o4 = o3 + SCONV_DIM
    o5 = o4 + SCONV_DIM
    pad = jnp.zeros((d, LANES - SSM_HEADS), F32)
    w_re = jnp.concatenate([w_in[:, :o2], w_in[:, o3:], w_in[:, o2:o3], pad], axis=1).astype(BF16)
    proj = norm_matmul(h.reshape(b * t, d), nw, w_re, 512).reshape(b, t, EVEN_PROJ)
    lane_pad = lambda v: jnp.concatenate([v.astype(F32), jnp.zeros((LANES - SSM_HEADS,), F32)]).reshape(1, LANES)
    aneg = lane_pad(-jnp.exp(a_log.astype(F32)) * LOG2E)
    dsk = jnp.repeat(d_skip.astype(F32), SSM_HEAD_DIM).reshape(1, SSM_INNER)
    del o4, o5
    return ssd_mixer(proj, h, conv_w, conv_b.reshape(1, -1), lane_pad(dt_bias), aneg, dsk,
                     gnorm_w.reshape(1, -1), sconv_w, w_out.astype(BF16))


def _qkv_body(x_ref, nw_ref, w_ref, cos_ref, sa_ref, sb_ref, qt_ref, k_ref, vt_ref):
    xn = _rms(x_ref[...], nw_ref[...]).astype(BF16)
    cos, sa, sb = cos_ref[...], sa_ref[...], sb_ref[...]

    def rope(a):
        return a * cos + pltpu.roll(a, ROPE_DIM // 2, 1) * sa + pltpu.roll(a, LANES - ROPE_DIM // 2, 1) * sb

    for c2 in range(0, ATT_HEADS, 2):
        proj = lambda part: jnp.dot(xn, w_ref[:, part * ATT_DIM + c2 * LANES:part * ATT_DIM + (c2 + 2) * LANES],
                                    preferred_element_type=F32)
        q2, k2, v2 = proj(0), proj(1), proj(2)
        for i in range(2):
            c = c2 + i
            lanes = slice(i * LANES, (i + 1) * LANES)
            qt_ref[0, c, 0] = (rope(q2[:, lanes]) * (ATT_HEAD_DIM ** -0.5 * LOG2E)).astype(BF16).T
            k_ref[:, c * LANES:(c + 1) * LANES] = rope(k2[:, lanes]).astype(BF16)
            vt_ref[0, c, 0, 0:LANES, :] = v2[:, lanes].astype(BF16).T
            row = lax.broadcasted_iota(jnp.int32, (VT_ROWS - LANES, ROW_TILE), 0)
            vt_ref[0, c, 0, LANES:VT_ROWS, :] = jnp.where(row == 0, 1.0, 0.0).astype(BF16)


def _rope_tables(t):
    half = ROPE_DIM // 2
    inv = 1.0 / (ROPE_THETA ** (jnp.arange(0, ROPE_DIM, 2, dtype=F32) / ROPE_DIM))
    ang = jnp.arange(t, dtype=F32)[:, None] * inv[None, :]
    cos, sin = jnp.cos(ang), jnp.sin(ang)
    rest = ATT_HEAD_DIM - ROPE_DIM
    one = jnp.ones((t, rest), F32)
    zero = jnp.zeros((t, rest), F32)
    zh = jnp.zeros((t, half), F32)
    cos64 = jnp.concatenate([cos, cos, one], axis=1)
    sa64 = jnp.concatenate([zh, sin, zero], axis=1)
    sb64 = jnp.concatenate([-sin, zh, zero], axis=1)
    tile2 = lambda v: jnp.concatenate([v, v], axis=1)
    return tile2(cos64), tile2(sa64), tile2(sb64)


def qkv_rope(x, nw, w_bf16, tables, b, t):
    n, d = x.shape
    assert ROW_TILE == ATT_TILE
    tiles = t // ROW_TILE
    tab = pl.BlockSpec((ROW_TILE, LANES), lambda i: (i % tiles, 0))
    tr = lambda rows: pl.BlockSpec((1, ATT_HEADS, 1, rows, ROW_TILE), lambda i: (i // tiles, 0, i % tiles, 0, 0))
    tr_shape = lambda rows: jax.ShapeDtypeStruct((b, ATT_HEADS, tiles, rows, ROW_TILE), BF16)
    return pl.pallas_call(
        _qkv_body,
        grid=(n // ROW_TILE,),
        in_specs=[pl.BlockSpec((ROW_TILE, d), lambda i: (i, 0)),
                  pl.BlockSpec((1, d), lambda i: (0, 0)),
                  pl.BlockSpec((d, 3 * ATT_DIM), lambda i: (0, 0)),
                  tab, tab, tab],
        out_specs=[tr(LANES), pl.BlockSpec((ROW_TILE, ATT_DIM), lambda i: (i, 0)), tr(VT_ROWS)],
        out_shape=[tr_shape(LANES), jax.ShapeDtypeStruct((n, ATT_DIM), BF16), tr_shape(VT_ROWS)],
        compiler_params=_params("parallel"),
        name="qkv_rope",
    )(x, nw.reshape(1, d), w_bf16, *tables)


def _attn_body(qt_ref, k_ref, vt_ref, lam_ref, sw_ref, o_ref, m_scr, acc_scr, *, lambda_init):
    tq = ATT_TILE
    qi = pl.program_id(2)
    feat = lax.broadcasted_iota(jnp.int32, (LANES, tq), 0)
    rk = lax.broadcasted_iota(jnp.int32, (tq, 2 * tq), 0)
    rq = lax.broadcasted_iota(jnp.int32, (tq, 2 * tq), 1)
    rq = jnp.where(rq >= tq, rq - tq, rq)
    visible = (rk // CHUNK) <= (rq // CHUNK)

    qs = []
    for g in range(ATT_GROUP):
        qt = qt_ref[0, g, 0]
        zero = jnp.zeros_like(qt)
        qs.append(jnp.concatenate([jnp.where(feat < ATT_HEAD_DIM, qt, zero),
                                   jnp.where(feat < ATT_HEAD_DIM, zero, qt)], axis=1))

    def scores(g, j, ntile):
        off = pl.multiple_of(j * tq, tq)
        return jnp.dot(k_ref[0, pl.ds(off, ntile * tq), g * LANES:(g + 1) * LANES], qs[g],
                       preferred_element_type=F32)

    def pv(g, j, ntile, p):
        out = jnp.dot(vt_ref[0, g, j], p[0:tq], preferred_element_type=F32)
        for i in range(1, ntile):
            out = out + jnp.dot(vt_ref[0, g, j + i], p[i * tq:(i + 1) * tq], preferred_element_type=F32)
        return out

    for g in range(ATT_GROUP):
        s = jnp.where(visible, scores(g, qi, 1), -jnp.inf)
        m0 = jnp.max(s, axis=0, keepdims=True)
        m_scr[g] = m0
        acc_scr[g] = pv(g, qi, 1, jnp.exp2(s - m0).astype(BF16))

    def update(j, ntile):
        for g in range(ATT_GROUP):
            s = scores(g, j, ntile)
            m_old = m_scr[g]
            m_new = jnp.maximum(m_old, jnp.max(s, axis=0, keepdims=True))
            m_scr[g] = m_new
            acc_scr[g] = jnp.exp2(m_old - m_new) * acc_scr[g] + pv(g, j, ntile, jnp.exp2(s - m_new).astype(BF16))

    @pl.when(qi % 2 == 1)
    def _():
        update(0, 1)

    @pl.when((qi // 2) % 2 == 1)
    def _():
        update(qi % 2, 2)

    def body(t, carry):
        update(qi % 4 + 4 * t, 4)
        return carry

    lax.fori_loop(0, qi // 4, body, 0)

    lp = lam_ref[...]
    lam = (jnp.exp(jnp.sum(lp[0:1] * lp[1:2], axis=-1, keepdims=True))
           - jnp.exp(jnp.sum(lp[2:3] * lp[3:4], axis=-1, keepdims=True)) + lambda_init)
    for g in range(ATT_GROUP):
        acc = acc_scr[g]
        on = acc[0:LANES, :] / acc[LANES:LANES + 1, :]
        o = (on[:, 0:tq] - lam * on[:, tq:2 * tq]).T
        o_ref[0, :, g * LANES:(g + 1) * LANES] = (_rms(o, sw_ref[...]) * (1.0 - lambda_init)).astype(o_ref.dtype)


def diff_attention(qt, k, vt, lam_p, subln_w, lambda_init):
    b, t, _ = k.shape
    tq = ATT_TILE
    tiles = t // tq
    g = ATT_GROUP
    return pl.pallas_call(
        functools.partial(_attn_body, lambda_init=lambda_init),
        grid=(b, ATT_HEADS // g, tiles),
        in_specs=[pl.BlockSpec((1, g, 1, LANES, tq), lambda i, h, j: (i, h, j, 0, 0)),
                  pl.BlockSpec((1, t, g * LANES), lambda i, h, j: (i, 0, h)),
                  pl.BlockSpec((1, g, tiles, VT_ROWS, tq), lambda i, h, j: (i, h, 0, 0, 0)),
                  pl.BlockSpec((4, ATT_HEAD_DIM), lambda i, h, j: (0, 0)),
                  pl.BlockSpec((1, LANES), lambda i, h, j: (0, 0))],
        out_specs=pl.BlockSpec((1, tq, g * LANES), lambda i, h, j: (i, j, h)),
        out_shape=jax.ShapeDtypeStruct((b, t, ATT_DIM), BF16),
        scratch_shapes=[pltpu.VMEM((g, 1, 2 * tq), F32), pltpu.VMEM((g, VT_ROWS, 2 * tq), F32)],
        compiler_params=_params("parallel", "parallel", "arbitrary"),
        name="diff_attention",
    )(qt, k, vt, lam_p, subln_w.reshape(1, LANES))


def _matmul_res_body(a_ref, w_ref, h_ref, o_ref):
    o_ref[...] = h_ref[...] + jnp.dot(a_ref[...], w_ref[...], preferred_element_type=F32)


def matmul_residual(a_bf16, w_bf16, h):
    n, kdim = a_bf16.shape
    d = w_bf16.shape[1]
    return pl.pallas_call(
        _matmul_res_body,
        grid=(n // ROW_TILE,),
        in_specs=[pl.BlockSpec((ROW_TILE, kdim), lambda i: (i, 0)),
                  pl.BlockSpec((kdim, d), lambda i: (0, 0)),
                  pl.BlockSpec((ROW_TILE, d), lambda i: (i, 0))],
        out_specs=pl.BlockSpec((ROW_TILE, d), lambda i: (i, 0)),
        out_shape=jax.ShapeDtypeStruct((n, d), F32),
        compiler_params=_params("parallel"),
        name="matmul_residual",
    )(a_bf16, w_bf16, h)


def odd_layer(h, nw, w_qkv, lam_p, subln_w, w_out, lambda_init, tables):
    b, t, d = h.shape
    qt, k, vt = qkv_rope(h.reshape(b * t, d), nw, w_qkv.astype(BF16), tables, b, t)
    o = diff_attention(qt, k.reshape(b, t, ATT_DIM), vt, lam_p.astype(F32), subln_w.astype(F32), lambda_init)
    return matmul_residual(o.reshape(b * t, ATT_DIM), w_out.astype(BF16), h.reshape(b * t, d)).reshape(b, t, d)


META_E, META_R, META_G = 0, 2, 4


def _router_body(h_ref, nw_ref, wr_ref, br_ref, before_ref, meta_ref, cnt_ref, run_cnt):
    tm = ROUTER_TILE

    @pl.when(pl.program_id(0) == 0)
    def _():
        run_cnt[...] = jnp.zeros_like(run_cnt)

    xn = _rms(h_ref[...], nw_ref[...])
    x_hi = xn.astype(BF16)
    x_lo = (xn - x_hi.astype(F32)).astype(BF16)
    logits = (jnp.dot(x_hi, wr_ref[0], preferred_element_type=F32)
              + (jnp.dot(x_lo, wr_ref[0], preferred_element_type=F32)
                 + jnp.dot(x_hi, wr_ref[1], preferred_element_type=F32))) + br_ref[...]
    lane = lax.broadcasted_iota(jnp.int32, (tm, LANES), 1)
    big = jnp.int32(LANES)
    neg = -jnp.inf

    is_g = lane < MOE_GROUPS
    gl = jnp.where(is_g, logits, neg)
    gexp = jnp.exp(gl - jnp.max(gl, axis=-1, keepdims=True))
    gprob = gexp / jnp.sum(gexp, axis=-1, keepdims=True)
    g_gate = jnp.max(gprob, axis=-1, keepdims=True)
    g_idx = jnp.min(jnp.where(is_g & (gprob == g_gate), lane, big), axis=-1, keepdims=True)

    lo = ROUTER_LANE0 + g_idx * EXPERTS_PER_GROUP
    sel = (lane >= lo) & (lane < lo + EXPERTS_PER_GROUP)
    el = jnp.where(sel, logits, neg)
    eexp = jnp.exp(el - jnp.max(el, axis=-1, keepdims=True))
    eprob = jnp.where(sel, eexp / jnp.sum(eexp, axis=-1, keepdims=True), -1.0)
    p1 = jnp.max(eprob, axis=-1, keepdims=True)
    i1 = jnp.min(jnp.where(eprob == p1, lane, big), axis=-1, keepdims=True)
    rest = jnp.where(lane == i1, -1.0, eprob)
    p2 = jnp.max(rest, axis=-1, keepdims=True)
    i2 = jnp.min(jnp.where(rest == p2, lane, big), axis=-1, keepdims=True)
    g1 = g_gate * p1 / (p1 + p2)
    g2 = g_gate * p2 / (p1 + p2)

    hit1 = lane == i1
    hit2 = lane == i2
    assign = jnp.where(hit1 | hit2, 1.0, 0.0)
    prior = jnp.dot(before_ref[...], assign.astype(BF16), preferred_element_type=F32) + run_cnt[...]
    r1 = jnp.sum(jnp.where(hit1, prior, 0.0), axis=-1, keepdims=True)
    r2 = jnp.sum(jnp.where(hit2, prior, 0.0), axis=-1, keepdims=True)
    run_cnt[...] = run_cnt[...] + jnp.sum(assign, axis=0, keepdims=True)
    cnt_ref[...] = jnp.broadcast_to(run_cnt[...], cnt_ref.shape)

    e1 = (i1 - ROUTER_LANE0).astype(F32)
    e2 = (i2 - ROUTER_LANE0).astype(F32)
    rec = jnp.zeros((tm, LANES), F32)
    for pos, val in ((META_E, e1), (META_E + 1, e2), (META_R, r1), (META_R + 1, r2), (META_G, g1), (META_G + 1, g2)):
        rec = jnp.where(lane == pos, val, rec)
    meta_ref[...] = rec


def moe_router(h, nw, w_router, b_router):
    n, d = h.shape
    tm = ROUTER_TILE
    before = jnp.tril(jnp.ones((tm, tm), BF16), -1)
    return pl.pallas_call(
        _router_body,
        grid=(n // tm,),
        in_specs=[pl.BlockSpec((tm, d), lambda i: (i, 0)),
                  pl.BlockSpec((1, d), lambda i: (0, 0)),
                  pl.BlockSpec((2, d, LANES), lambda i: (0, 0, 0)),
                  pl.BlockSpec((1, LANES), lambda i: (0, 0)),
                  pl.BlockSpec((tm, tm), lambda i: (0, 0))],
        out_specs=[pl.BlockSpec((tm, LANES), lambda i: (i, 0)),
                   pl.BlockSpec((SUBLANES, LANES), lambda i: (0, 0))],
        out_shape=[jax.ShapeDtypeStruct((n, LANES), F32), jax.ShapeDtypeStruct((SUBLANES, LANES), F32)],
        scratch_shapes=[pltpu.VMEM((1, LANES), F32)],
        compiler_params=_params("arbitrary"),
        name="moe_router",
    )(h, nw.reshape(1, d), w_router, b_router, before)


def _row_copy(src, src_row, dst, dst_row, sem):
    return pltpu.make_async_copy(src.at[pl.ds(src_row, 1), :], dst.at[pl.ds(dst_row, 1), :], sem)


def _pack_rows(x):
    bits = lax.bitcast_convert_type(x.astype(BF16).astype(F32), jnp.uint32)
    return (bits[:, :PACKED] >> 16) | bits[:, PACKED:]


def _unpack_rows(u):
    lo = lax.bitcast_convert_type(u << 16, F32)
    hi = lax.bitcast_convert_type(u & jnp.uint32(0xFFFF0000), F32)
    return lo, hi


def _dispatch_body(pends_ref, padded_ref, dest_ref, h_ref, nw_ref, xrows_ref, xp_scr, zbuf, sem, zsem):
    tm = MOE_TILE

    @pl.when(pl.program_id(0) == 0)
    def _():
        zbuf[...] = jnp.zeros_like(zbuf)

        def tail(e):
            return xrows_ref.at[pl.ds(pl.multiple_of(pends_ref[e] - MOE_BLOCK, MOE_BLOCK), MOE_BLOCK), :]

        def fill(e, carry):
            @pl.when(padded_ref[e] > 0)
            def _():
                pltpu.make_async_copy(zbuf, tail(e), zsem).start()
            return carry

        def drain(e, carry):
            @pl.when(padded_ref[e] > 0)
            def _():
                pltpu.make_async_copy(zbuf, tail(e), zsem).wait()
            return carry

        def block(bk):
            return xrows_ref.at[pl.ds(pl.multiple_of(bk * MOE_BLOCK, MOE_BLOCK), MOE_BLOCK), :]

        def fill_block(bk, carry):
            pltpu.make_async_copy(zbuf, block(bk), zsem).start()
            return carry

        def drain_block(bk, carry):
            pltpu.make_async_copy(zbuf, block(bk), zsem).wait()
            return carry

        first_unused = pends_ref[N_EXPERTS - 1] // MOE_BLOCK
        n_blk = xrows_ref.shape[0] // MOE_BLOCK
        lax.fori_loop(0, N_EXPERTS, fill, 0)
        lax.fori_loop(first_unused, n_blk, fill_block, 0)
        lax.fori_loop(0, N_EXPERTS, drain, 0)
        lax.fori_loop(first_unused, n_blk, drain_block, 0)

    xp_scr[...] = _pack_rows(_rms(h_ref[...], nw_ref[...]))

    def issue(i, carry):
        _row_copy(xp_scr, i, xrows_ref, dest_ref[0, 0, 2 * i], sem).start()
        _row_copy(xp_scr, i, xrows_ref, dest_ref[0, 0, 2 * i + 1], sem).start()
        return carry

    lax.fori_loop(0, tm, issue, 0, unroll=DMA_UNROLL)
    for _ in range(2):
        pltpu.make_async_copy(xp_scr, xrows_ref.at[pl.ds(0, tm), :], sem).wait()


def moe_dispatch(h, nw, dest, pends, padded, n_rows):
    n, d = h.shape
    tm = MOE_TILE
    return pl.pallas_call(
        _dispatch_body,
        grid_spec=pltpu.PrefetchScalarGridSpec(
            num_scalar_prefetch=2,
            grid=(n // tm,),
            in_specs=[pl.BlockSpec((1, 1, 2 * tm), lambda i, pe, pa: (i, 0, 0), memory_space=pltpu.SMEM),
                      pl.BlockSpec((tm, d), lambda i, pe, pa: (i, 0)),
                      pl.BlockSpec((1, d), lambda i, pe, pa: (0, 0))],
            out_specs=pl.BlockSpec(memory_space=pl.ANY),
            scratch_shapes=[pltpu.VMEM((tm, PACKED), jnp.uint32), pltpu.VMEM((MOE_BLOCK, PACKED), jnp.uint32),
                            pltpu.SemaphoreType.DMA, pltpu.SemaphoreType.DMA]),
        out_shape=jax.ShapeDtypeStruct((n_rows, PACKED), jnp.uint32),
        compiler_params=_params("arbitrary"),
        name="moe_dispatch",
    )(pends, padded, dest.reshape(n // tm, 1, 2 * tm), h, nw.reshape(1, d))


def _ffn_body(pstart_ref, nblk_ref, w1_ref, w3_ref, w2_ref, xrows_ref, yrows_ref,
              w1b, w3b, w2b, xbuf, ybuf, xsem, ysem):
    e = pl.program_id(0)
    nb = nblk_ref[e]
    g0 = pstart_ref[e] // MOE_BLOCK
    n_used = (pstart_ref[N_EXPERTS - 1] // MOE_BLOCK) + nblk_ref[N_EXPERTS - 1]
    n_blk = yrows_ref.shape[0] // MOE_BLOCK

    def rows(g):
        return pl.ds(pl.multiple_of(g * MOE_BLOCK, MOE_BLOCK), MOE_BLOCK)

    def x_copy(g):
        slot = g % (FFN_AHEAD + 1)
        return pltpu.make_async_copy(xrows_ref.at[rows(g), :], xbuf.at[slot], xsem.at[slot])

    def y_copy(g):
        slot = g % 2
        return pltpu.make_async_copy(ybuf.at[slot], yrows_ref.at[rows(g), :], ysem.at[slot])

    @pl.when(e == 0)
    def _():
        for g in range(FFN_AHEAD):
            @pl.when(g < n_used)
            def _():
                x_copy(g).start()

    @pl.when(nb > 0)
    def _():
        w1b[...] = w1_ref[0, 0].astype(BF16)
        w3b[...] = w3_ref[0, 0].astype(BF16)
        w2b[...] = w2_ref[0, 0].astype(BF16)

        def block(i, carry):
            g = g0 + i
            x_copy(g).wait()

            @pl.when(g + FFN_AHEAD < n_used)
            def _():
                x_copy(g + FFN_AHEAD).start()

            @pl.when(g >= 2)
            def _():
                y_copy(g - 2).wait()

            lo, hi = _unpack_rows(xbuf[g % (FFN_AHEAD + 1)])
            x = jnp.concatenate([lo.astype(BF16), hi.astype(BF16)], axis=1)
            u = jnp.dot(x, w1b[...], preferred_element_type=F32)
            gate = jnp.dot(x, w3b[...], preferred_element_type=F32)
            ybuf[g % 2] = _pack_rows(jnp.dot((_silu(u) * gate).astype(BF16), w2b[...], preferred_element_type=F32))
            y_copy(g).start()
            return carry

        lax.fori_loop(0, nb, block, 0)

    @pl.when(e == N_EXPERTS - 1)
    def _():
        @pl.when(n_used >= 2)
        def _():
            y_copy(n_used - 2).wait()

        y_copy(n_used - 1).wait()

        ybuf[0] = jnp.zeros((MOE_BLOCK, PACKED), jnp.uint32)

        def tail(bk):
            return pltpu.make_async_copy(ybuf.at[0], yrows_ref.at[rows(bk), :], ysem.at[0])

        def fill(bk, carry):
            tail(bk).start()
            return carry

        def drain(bk, carry):
            tail(bk).wait()
            return carry

        lax.fori_loop(n_used, n_blk, fill, 0)
        lax.fori_loop(n_used, n_blk, drain, 0)


def moe_ffn(x_rows, pstarts, nblk, w1, w3, w2, layer):
    n_rows = x_rows.shape[0]
    d, de = w1.shape[-2:]
    wsel = lambda e, ps, nb: (layer, e, 0, 0)
    return pl.pallas_call(
        _ffn_body,
        grid_spec=pltpu.PrefetchScalarGridSpec(
            num_scalar_prefetch=2,
            grid=(N_EXPERTS,),
            in_specs=[pl.BlockSpec((1, 1, d, de), wsel),
                      pl.BlockSpec((1, 1, d, de), wsel),
                      pl.BlockSpec((1, 1, de, d), wsel),
                      pl.BlockSpec(memory_space=pl.ANY)],
            out_specs=pl.BlockSpec(memory_space=pl.ANY),
            scratch_shapes=[pltpu.VMEM((d, de), BF16), pltpu.VMEM((d, de), BF16), pltpu.VMEM((de, d), BF16),
                            pltpu.VMEM((FFN_AHEAD + 1, MOE_BLOCK, PACKED), jnp.uint32),
                            pltpu.VMEM((2, MOE_BLOCK, PACKED), jnp.uint32),
                            pltpu.SemaphoreType.DMA((FFN_AHEAD + 1,)), pltpu.SemaphoreType.DMA((2,))]),
        out_shape=jax.ShapeDtypeStruct((n_rows, PACKED), jnp.uint32),
        compiler_params=_params("arbitrary"),
        name="moe_ffn",
    )(pstarts, nblk, w1, w3, w2, x_rows)


def _combine_body(dest_ref, meta_ref, h_ref, yrows_ref, fw_ref, o_ref, ybuf, sem, *, final_norm):
    tm = MOE_TILE

    def issue(i, carry):
        _row_copy(yrows_ref, dest_ref[0, 0, 2 * i], ybuf.at[0], i, sem).start()
        _row_copy(yrows_ref, dest_ref[0, 0, 2 * i + 1], ybuf.at[1], i, sem).start()
        return carry

    lax.fori_loop(0, tm, issue, 0, unroll=DMA_UNROLL)
    for half in range(2):
        pltpu.make_async_copy(yrows_ref.at[pl.ds(0, tm), :], ybuf.at[half], sem).wait()

    meta = meta_ref[...]
    g1 = meta[:, META_G:META_G + 1]
    g2 = meta[:, META_G + 1:META_G + 2]
    lo1, hi1 = _unpack_rows(ybuf[0])
    lo2, hi2 = _unpack_rows(ybuf[1])
    out = h_ref[...] + jnp.concatenate([g1 * lo1 + g2 * lo2, g1 * hi1 + g2 * hi2], axis=1)
    if final_norm:
        out = _rms(out, fw_ref[...])
    o_ref[...] = out


def moe_combine(h, meta, dest, y_rows, final_w):
    n, d = h.shape
    tm = MOE_TILE
    final_norm = final_w is not None
    fw = (final_w if final_norm else jnp.ones((d,), F32)).reshape(1, d)
    return pl.pallas_call(
        functools.partial(_combine_body, final_norm=final_norm),
        grid=(n // tm,),
        in_specs=[pl.BlockSpec((1, 1, 2 * tm), lambda i: (i, 0, 0), memory_space=pltpu.SMEM),
                  pl.BlockSpec((tm, LANES), lambda i: (i, 0)),
                  pl.BlockSpec((tm, d), lambda i: (i, 0)),
                  pl.BlockSpec(memory_space=pl.ANY),
                  pl.BlockSpec((1, d), lambda i: (0, 0))],
        out_specs=pl.BlockSpec((tm, d), lambda i: (i, 0)),
        out_shape=jax.ShapeDtypeStruct((n, d), F32),
        scratch_shapes=[pltpu.VMEM((2, tm, PACKED), jnp.uint32), pltpu.SemaphoreType.DMA],
        compiler_params=_params("arbitrary"),
        name="moe_combine",
    )(dest.reshape(n // tm, 1, 2 * tm), meta, h, y_rows, fw)


def hier_moe_layer(h, nw, w_group, b_group, w_expert, b_expert, w1, w3, w2, layer, final_w):
    n, d = h.shape
    pad_cols = LANES - MOE_GROUPS - N_EXPERTS
    w_router = jnp.concatenate([w_group, w_expert, jnp.zeros((d, pad_cols), F32)], axis=1)
    w_hi = w_router.astype(BF16)
    w_router = jnp.stack([w_hi, (w_router - w_hi.astype(F32)).astype(BF16)])
    b_router = jnp.concatenate([b_group, b_expert, jnp.zeros((pad_cols,), F32)]).reshape(1, LANES)
    meta, cnt = moe_router(h, nw, w_router, b_router)

    counts = cnt[0, ROUTER_LANE0:ROUTER_LANE0 + N_EXPERTS].astype(jnp.int32)
    padded = (counts + MOE_BLOCK - 1) // MOE_BLOCK * MOE_BLOCK
    pends = jnp.cumsum(padded)
    pstarts = pends - padded
    expert = meta[:, META_E:META_E + 2].astype(jnp.int32)
    rank = meta[:, META_R:META_R + 2].astype(jnp.int32)
    hit = expert[:, :, None] == jnp.arange(N_EXPERTS, dtype=jnp.int32)
    dest = (jnp.sum(jnp.where(hit, pstarts, 0), axis=-1) + rank).reshape(n * 2)
    n_rows = n * 2 + N_EXPERTS * MOE_BLOCK

    x_rows = moe_dispatch(h, nw, dest, pends.astype(jnp.int32), padded.astype(jnp.int32), n_rows)
    y_rows = moe_ffn(x_rows, pstarts.astype(jnp.int32), (padded // MOE_BLOCK).astype(jnp.int32), w1, w3, w2, layer)
    return moe_combine(h, meta, dest, y_rows, final_w)


def _lambda_init(layer):
    return 0.8 - 0.6 * math.exp(-0.3 * layer)


def kernel(x, norm_mix_w, norm_ffn_w, final_norm_w, ev_w_in, ev_conv_w, ev_conv_b, ev_dt_bias, ev_a_log,
           ev_d_skip, ev_gnorm_w, ev_sconv_w, ev_w_out, od_w_qkv, od_lambda, od_subln_w, od_w_out,
           moe_w_group, moe_b_group, moe_w_expert, moe_b_expert, moe_w1, moe_w3, moe_w2):
    b, t, d = x.shape
    depth = norm_mix_w.shape[0]
    tables = _rope_tables(t)
    h = x
    for layer in range(depth):
        i = layer // 2
        if layer % 2 == 0:
            h = even_layer(h, norm_mix_w[layer], ev_w_in[i], ev_conv_w[i], ev_conv_b[i], ev_dt_bias[i],
                           ev_a_log[i], ev_d_skip[i], ev_gnorm_w[i], ev_sconv_w[i], ev_w_out[i])
        else:
            h = odd_layer(h, norm_mix_w[layer], od_w_qkv[i], od_lambda[i], od_subln_w[i], od_w_out[i],
                          _lambda_init(layer), tables)
        final_w = final_norm_w if layer == depth - 1 else None
        h = hier_moe_layer(h.reshape(b * t, d), norm_ffn_w[layer], moe_w_group[layer], moe_b_group[layer],
                           moe_w_expert[layer], moe_b_expert[layer], moe_w1, moe_w3, moe_w2, layer,
                           final_w).reshape(b, t, d)
    return h
```
